```python
import jax, jax.numpy as jnp
from jax import lax
import numpy as np


D_MODEL = 4096
BATCH = 1
SEQ = 8192
DEPTH = 1

CHUNK = 64
Q_BLOCK = 128
ML_HEADS = 8
ML_V_DIM = 256
ML_QK_DIM = 128
ML_WIDTH = ML_HEADS * ML_V_DIM
ML_QK_WIDTH = ML_HEADS * ML_QK_DIM
SB_HEADS = 16
SB_HEAD_DIM = 128
SB_WIDTH = SB_HEADS * SB_HEAD_DIM
CONV_WIDTH = 4
D_FF = 11008
N_BRANCH = 2
EPS = 1e-6
IN_SIZES = (ML_QK_WIDTH, ML_QK_WIDTH, ML_WIDTH, ML_WIDTH, ML_HEADS, ML_HEADS,
            SB_WIDTH, SB_WIDTH, SB_WIDTH, D_MODEL, D_MODEL)
D_IN = 2 * ML_QK_WIDTH + 2 * ML_WIDTH + 2 * ML_HEADS + 3 * SB_WIDTH + N_BRANCH * D_MODEL

kernel_name = 'hybrid_mlstm_stickbreak_macaron'


def rmsnorm(x, g):
    xf = x.astype(jnp.float32)
    y = xf * lax.rsqrt(jnp.mean(xf * xf, axis=-1, keepdims=True) + EPS)
    return (y * g.astype(jnp.float32)).astype(x.dtype)


def swiglu(h, w1, w3, w2):
    return (jax.nn.silu(h @ w1) * (h @ w3)) @ w2


def split_cols(p, sizes):
    offs = []
    acc = 0
    for s in sizes[:-1]:
        acc += s
        offs.append(acc)
    return jnp.split(p, offs, axis=-1)


def causal_depthwise_conv(x, w):
    k = w.shape[0]
    return lax.conv_general_dilated(
        x, w[:, None, :].astype(x.dtype), window_strides=(1,), padding=[(k - 1, 0)],
        dimension_numbers=('NWC', 'WIO', 'NWC'), feature_group_count=x.shape[-1])


def mlstm(q, k, v, log_i, log_f):
    B, S, H, dk = q.shape
    dv = v.shape[-1]
    nc = S // CHUNK
    k = k * (dk ** -0.5)

    def chunks(a):
        a = a.reshape((B, nc, CHUNK) + a.shape[2:])
        return jnp.moveaxis(jnp.moveaxis(a, 1, 0), 3, 2)

    tril = jnp.tril(jnp.ones((CHUNK, CHUNK), dtype=bool))

    def step(carry, inp):
        C, n, m = carry
        qc, kc, vc, ic, fc = inp
        b = jnp.cumsum(fc, axis=-1)
        log_d = jnp.where(tril, b[..., :, None] - b[..., None, :] + ic[..., None, :], -jnp.inf)
        m_inter = b + m[..., None]
        m_t = jnp.maximum(m_inter, jnp.max(log_d, axis=-1))
        w = jnp.einsum('bhtd,bhsd->bhts', qc, kc) * jnp.exp(log_d - m_t[..., None])
        decay = jnp.exp(m_inter - m_t)
        num = (decay[..., None] * jnp.einsum('bhtd,bhdv->bhtv', qc, C)
               + jnp.einsum('bhts,bhsv->bhtv', w, vc))
        den = decay * jnp.einsum('bhtd,bhd->bht', qc, n) + jnp.sum(w, axis=-1)
        h = num / jnp.maximum(jnp.abs(den), jnp.exp(-m_t))[..., None]
        b_last = b[..., -1]
        a = b_last[..., None] - b + ic
        m_new = jnp.maximum(b_last + m, jnp.max(a, axis=-1))
        carry_scale = jnp.exp(b_last + m - m_new)
        src = jnp.exp(a - m_new[..., None])
        C = carry_scale[..., None, None] * C + jnp.einsum('bhs,bhsd,bhsv->bhdv', src, kc, vc)
        n = carry_scale[..., None] * n + jnp.einsum('bhs,bhsd->bhd', src, kc)
        return (C, n, m_new), h

    init = (jnp.zeros((B, H, dk, dv), q.dtype), jnp.zeros((B, H, dk), q.dtype),
            jnp.zeros((B, H), q.dtype))
    _, hs = lax.scan(step, init, (chunks(q), chunks(k), chunks(v), chunks(log_i), chunks(log_f)))
    hs = jnp.moveaxis(jnp.moveaxis(hs, 2, 3), 0, 1)
    return hs.reshape(B, S, H, dv)


def stick_breaking(q, k, v):
    B, S, H, d = q.shape
    nb = S // Q_BLOCK
    q = jnp.transpose(q, (0, 2, 1, 3)) * (d ** -0.5)
    k = jnp.transpose(k, (0, 2, 1, 3))
    v = jnp.transpose(v, (0, 2, 1, 3))
    q_blocks = jnp.transpose(q.reshape(B, H, nb, Q_BLOCK, d), (2, 0, 1, 3, 4))
    starts = jnp.arange(nb, dtype=jnp.int32) * Q_BLOCK
    key_pos = jnp.arange(S, dtype=jnp.int32)

    def block(args):
        qb, start = args
        z = jnp.einsum('bhqd,bhkd->bhqk', qb, k)
        qpos = start + jnp.arange(Q_BLOCK, dtype=jnp.int32)
        strict = key_pos[None, :] < qpos[:, None]
        log_beta = jax.nn.log_sigmoid(z)
        log_rest = jnp.where(strict, jax.nn.log_sigmoid(-z), 0.0)
        after = lax.cumsum(log_rest, axis=3, reverse=True) - log_rest
        weights = jnp.where(strict, jnp.exp(log_beta + after), 0.0)
        return jnp.einsum('bhqk,bhkd->bhqd', weights, v)

    out = lax.map(block, (q_blocks, starts))
    return jnp.transpose(out, (1, 0, 3, 2, 4)).reshape(B, S, H, d)


def setup_inputs(seed: int = 0) -> dict:
    key = jax.random.key(seed)
    ks = jax.random.split(key, 20)
    f32 = jnp.float32

    def w(k, shape, fan_in):
        return jax.random.normal(k, shape, f32) * (fan_in ** -0.5)

    def gain(k, shape):
        return 1.0 + 0.01 * jax.random.normal(k, shape, f32)

    return {
        'x': jax.random.normal(ks[0], (BATCH, SEQ, D_MODEL), f32),
        'g_ffn1': gain(ks[1], (DEPTH, D_MODEL)),
        'w1_ffn1': w(ks[2], (DEPTH, D_MODEL, D_FF), D_MODEL),
        'w3_ffn1': w(ks[3], (DEPTH, D_MODEL, D_FF), D_MODEL),
        'w2_ffn1': w(ks[4], (DEPTH, D_FF, D_MODEL), D_FF),
        'g_mix': gain(ks[5], (DEPTH, D_MODEL)),
        'w_in': w(ks[6], (DEPTH, D_MODEL, D_IN), D_MODEL),
        'conv_qk': w(ks[7], (DEPTH, CONV_WIDTH, 2 * ML_QK_WIDTH), CONV_WIDTH),
        'b_igate': 0.1 * jax.random.normal(ks[8], (DEPTH, ML_HEADS), f32),
        'b_fgate': 3.0 + 0.5 * jax.random.normal(ks[9], (DEPTH, ML_HEADS), f32),
        'g_mlstm_out': gain(ks[10], (DEPTH, ML_WIDTH)),
        'w_proj_a': w(ks[11], (DEPTH, ML_WIDTH, D_MODEL), ML_WIDTH),
        'w_proj_b': w(ks[12], (DEPTH, SB_WIDTH, D_MODEL), SB_WIDTH),
        'w_out': w(ks[13], (DEPTH, D_MODEL, D_MODEL), D_MODEL),
        'g_ffn2': gain(ks[14], (DEPTH, D_MODEL)),
        'w1_ffn2': w(ks[15], (DEPTH, D_MODEL, D_FF), D_MODEL),
        'w3_ffn2': w(ks[16], (DEPTH, D_MODEL, D_FF), D_MODEL),
        'w2_ffn2': w(ks[17], (DEPTH, D_FF, D_MODEL), D_FF),
        'g_final': gain(ks[18], (D_MODEL,)),
    }


def reference(x, g_ffn1, w1_ffn1, w3_ffn1, w2_ffn1, g_mix, w_in, conv_qk, b_igate, b_fgate,
              g_mlstm_out, w_proj_a, w_proj_b, w_out, g_ffn2, w1_ffn2, w3_ffn2, w2_ffn2, g_final):
    B, S, _ = x.shape
    f32 = jnp.float32
    for l in range(DEPTH):
        x = x + 0.5 * swiglu(rmsnorm(x, g_ffn1[l]), w1_ffn1[l], w3_ffn1[l], w2_ffn1[l])

        h = rmsnorm(x, g_mix[l])
        p = h @ w_in[l]
        mq, mk, mv, mo, mi, mf, sq, sk, sv, ga, gb = split_cols(p, IN_SIZES)

        qk = jax.nn.silu(causal_depthwise_conv(jnp.concatenate([mq, mk], axis=-1), conv_qk[l]))
        mq, mk = jnp.split(qk.astype(f32), 2, axis=-1)
        log_i = mi.astype(f32) + b_igate[l].astype(f32)
        log_f = jax.nn.log_sigmoid(mf.astype(f32) + b_fgate[l].astype(f32))
        ya = mlstm(mq.reshape(B, S, ML_HEADS, ML_QK_DIM), mk.reshape(B, S, ML_HEADS, ML_QK_DIM),
                   mv.astype(f32).reshape(B, S, ML_HEADS, ML_V_DIM), log_i, log_f)
        ya = ya * lax.rsqrt(jnp.mean(ya * ya, axis=-1, keepdims=True) + EPS)
        ya = ya * g_mlstm_out[l].astype(f32).reshape(ML_HEADS, ML_V_DIM)
        ya = (ya.reshape(B, S, ML_WIDTH) * jax.nn.sigmoid(mo.astype(f32))).astype(x.dtype)

        yb = stick_breaking(sq.astype(f32).reshape(B, S, SB_HEADS, SB_HEAD_DIM),
                            sk.astype(f32).reshape(B, S, SB_HEADS, SB_HEAD_DIM),
                            sv.astype(f32).reshape(B, S, SB_HEADS, SB_HEAD_DIM))
        yb = yb.reshape(B, S, SB_WIDTH).astype(x.dtype)

        merged = jax.nn.sigmoid(ga) * (ya @ w_proj_a[l]) + jax.nn.sigmoid(gb) * (yb @ w_proj_b[l])
        x = x + merged @ w_out[l]

        x = x + 0.5 * swiglu(rmsnorm(x, g_ffn2[l]), w1_ffn2[l], w3_ffn2[l], w2_ffn2[l])
    return rmsnorm(x, g_final)
```

```python
import functools

import jax
import jax.numpy as jnp
from jax import lax
from jax.experimental import pallas as pl
from jax.experimental.pallas import tpu as pltpu

F32 = jnp.float32
BF16 = jnp.bfloat16
EPS = 1e-6

V7X_VMEM_BYTES = 64 * 1024 * 1024
VMEM_LIMIT_BYTES = V7X_VMEM_BYTES - 8 * 1024 * 1024
LANES = 128

ML_HEADS = 8
ML_QK_DIM = 128
ML_V_DIM = 256
ML_CHUNK = 64
SB_HEADS = 16
SB_HEAD_DIM = 128
CONV_WIDTH = 4
SB_DONE_LOG = -100.0


def _params(n_grid):
    return pltpu.CompilerParams(
        dimension_semantics=("arbitrary",) * n_grid,
        vmem_limit_bytes=VMEM_LIMIT_BYTES)


def _pick(n, candidates):
    for c in candidates:
        if n % c == 0:
            return c
    raise ValueError(f"no tile in {candidates} divides {n}")


def _log_sigmoid(x):
    return jnp.minimum(x, 0.0) - jnp.log1p(jnp.exp(-jnp.abs(x)))


def _rmsnorm_kernel(x_ref, g_ref, o_ref):
    x = x_ref[...]
    ms = jnp.mean(x * x, axis=-1, keepdims=True)
    o_ref[...] = (x * lax.rsqrt(ms + EPS) * g_ref[...]).astype(o_ref.dtype)


def rmsnorm(x, g, out_dtype):
    s, d = x.shape
    tr = _pick(s, (256, 128, 64, 8))
    return pl.pallas_call(
        _rmsnorm_kernel,
        grid=(s // tr,),
        in_specs=[pl.BlockSpec((tr, d), lambda i: (i, 0)),
                  pl.BlockSpec((1, d), lambda i: (0, 0))],
        out_specs=pl.BlockSpec((tr, d), lambda i: (i, 0)),
        out_shape=jax.ShapeDtypeStruct((s, d), out_dtype),
        compiler_params=_params(1),
        name="rmsnorm",
    )(x, g.reshape(1, d).astype(F32))


def _dot(a, b):
    return jnp.dot(a, b, preferred_element_type=F32)


def _gateup_kernel(a_ref, w1_ref, w3_ref, o_ref):
    a = a_ref[...]
    g = _dot(a, w1_ref[...])
    u = _dot(a, w3_ref[...])
    o_ref[...] = (g * jax.nn.sigmoid(g) * u).astype(o_ref.dtype)


def gateup(h, w1, w3):
    m, k = h.shape
    n = w1.shape[1]
    tm = _pick(m, (1024, 512, 256, 128, 64, 8))
    tn = _pick(n, (256, 128))
    return pl.pallas_call(
        _gateup_kernel,
        grid=(m // tm, n // tn),
        in_specs=[pl.BlockSpec((tm, k), lambda i, j: (i, 0)),
                  pl.BlockSpec((k, tn), lambda i, j: (0, j)),
                  pl.BlockSpec((k, tn), lambda i, j: (0, j))],
        out_specs=pl.BlockSpec((tm, tn), lambda i, j: (i, j)),
        out_shape=jax.ShapeDtypeStruct((m, n), BF16),
        compiler_params=_params(2),
        name="ffn_gateup",
    )(h, w1, w3)


def _resid_kernel(a_ref, w_ref, x_ref, o_ref, *, alpha):
    o_ref[...] = x_ref[...] + alpha * _dot(a_ref[...], w_ref[...])


def matmul_residual(a, w, x, alpha):
    m, k = a.shape
    n = w.shape[1]
    tm = _pick(m, (512, 256, 128, 64, 8))
    tn = _pick(n, (256, 128))
    return pl.pallas_call(
        functools.partial(_resid_kernel, alpha=alpha),
        grid=(m // tm, n // tn),
        in_specs=[pl.BlockSpec((tm, k), lambda i, j: (i, 0)),
                  pl.BlockSpec((k, tn), lambda i, j: (0, j)),
                  pl.BlockSpec((tm, tn), lambda i, j: (i, j))],
        out_specs=pl.BlockSpec((tm, tn), lambda i, j: (i, j)),
        out_shape=jax.ShapeDtypeStruct((m, n), F32),
        compiler_params=_params(2),
        name="matmul_residual",
    )(a, w, x)


def _proj_kernel(a_ref, w_ref, s_ref, o_ref):
    o_ref[...] = (_dot(a_ref[...], w_ref[...]) * s_ref[...]).astype(o_ref.dtype)


def project(a, w, col_scale, out_dtype):
    m, k = a.shape
    n = w.shape[1]
    tm = _pick(m, (1024, 512, 256, 128, 64, 8))
    tn = _pick(n, (256, 128))
    return pl.pallas_call(
        _proj_kernel,
        grid=(m // tm, n // tn),
        in_specs=[pl.BlockSpec((tm, k), lambda i, j: (i, 0)),
                  pl.BlockSpec((k, tn), lambda i, j: (0, j)),
                  pl.BlockSpec((1, tn), lambda i, j: (0, j))],
        out_specs=pl.BlockSpec((tm, tn), lambda i, j: (i, j)),
        out_shape=jax.ShapeDtypeStruct((m, n), out_dtype),
        compiler_params=_params(2),
        name="in_project",
    )(a, w, col_scale.reshape(1, n).astype(F32))


def _merge_kernel(ya_ref, yb_ref, wa_ref, wb_ref, ga_ref, gb_ref, o_ref):
    pa = _dot(ya_ref[...], wa_ref[...])
    pb = _dot(yb_ref[...], wb_ref[...])
    o_ref[...] = (jax.nn.sigmoid(ga_ref[...]) * pa
                  + jax.nn.sigmoid(gb_ref[...]) * pb).astype(o_ref.dtype)


def gated_merge(ya, yb, wa, wb, gates, ga_col0, gb_col0):
    m, k = ya.shape
    n = wa.shape[1]
    tm = _pick(m, (1024, 512, 256, 128, 64, 8))
    tn = _pick(n, (256, 128))
    ga_blk, gb_blk = ga_col0 // tn, gb_col0 // tn
    return pl.pallas_call(
        _merge_kernel,
        grid=(m // tm, n // tn),
        in_specs=[pl.BlockSpec((tm, k), lambda i, j: (i, 0)),
                  pl.BlockSpec((tm, k), lambda i, j: (i, 0)),
                  pl.BlockSpec((k, tn), lambda i, j: (0, j)),
                  pl.BlockSpec((k, tn), lambda i, j: (0, j)),
                  pl.BlockSpec((tm, tn), lambda i, j: (i, ga_blk + j)),
                  pl.BlockSpec((tm, tn), lambda i, j: (i, gb_blk + j))],
        out_specs=pl.BlockSpec((tm, tn), lambda i, j: (i, j)),
        out_shape=jax.ShapeDtypeStruct((m, n), BF16),
        compiler_params=_params(2),
        name="gated_merge",
    )(ya, yb, wa, wb, gates, gates)


def _conv_kernel(cur_ref, prev_ref, w_ref, q_ref, k_ref, *, k_scale):
    cur = cur_ref[...]
    prev = jnp.where(pl.program_id(0) > 0, prev_ref[...], 0.0)
    ext = jnp.concatenate([prev, cur], axis=0)
    w = w_ref[...]
    tr = cur.shape[0]
    acc = w[CONV_WIDTH - 1:CONV_WIDTH] * cur
    for j in range(CONV_WIDTH - 1):
        off = 8 - (CONV_WIDTH - 1) + j
        acc = acc + w[j:j + 1] * ext[off:off + tr]
    y = acc * jax.nn.sigmoid(acc)
    half = y.shape[1] // 2
    q_ref[...] = y[:, :half].astype(q_ref.dtype)
    k_ref[...] = (y[:, half:] * k_scale).astype(k_ref.dtype)


def conv_silu_qk(p, conv_w, width2, k_scale):
    s = p.shape[0]
    tr = _pick(s, (256, 128, 64, 8))
    half = width2 // 2
    return pl.pallas_call(
        functools.partial(_conv_kernel, k_scale=k_scale),
        grid=(s // tr,),
        in_specs=[pl.BlockSpec((tr, width2), lambda i: (i, 0)),
                  pl.BlockSpec((8, width2), lambda i: (jnp.maximum(i * (tr // 8) - 1, 0), 0)),
                  pl.BlockSpec((CONV_WIDTH, width2), lambda i: (0, 0))],
        out_specs=[pl.BlockSpec((tr, half), lambda i: (i, 0)),
                   pl.BlockSpec((tr, half), lambda i: (i, 0))],
        out_shape=[jax.ShapeDtypeStruct((s, half), BF16),
                   jax.ShapeDtypeStruct((s, half), BF16)],
        compiler_params=_params(1),
        name="conv_silu_qk",
    )(p, p, conv_w.astype(F32))


def _gate_kernel(g_ref, bi_ref, bf_ref, b_ref, li_ref, r_ref, *, chunk):
    g = g_ref[...]
    li = g[:, :LANES] + bi_ref[...]
    lf = _log_sigmoid(g[:, LANES:] + bf_ref[...])
    pos = lax.broadcasted_iota(jnp.int32, lf.shape, 0) % chunk
    b = lf
    d = 1
    while d < chunk:
        b = b + jnp.where(pos >= d, pltpu.roll(b, d, 0), 0.0)
        d *= 2
    b_ref[...] = b
    li_ref[...] = li
    r_ref[...] = li - b


def gate_prep(g, b_i, b_f, chunk):
    s = g.shape[0]
    tr = _pick(s, (512, 256, 128, 64))
    pad = lambda v: jnp.zeros((1, LANES), F32).at[0, :v.shape[0]].set(v.astype(F32))
    shp = jax.ShapeDtypeStruct((s, LANES), F32)
    return pl.pallas_call(
        functools.partial(_gate_kernel, chunk=chunk),
        grid=(s // tr,),
        in_specs=[pl.BlockSpec((tr, 2 * LANES), lambda i: (i, 0)),
                  pl.BlockSpec((1, LANES), lambda i: (0, 0)),
                  pl.BlockSpec((1, LANES), lambda i: (0, 0))],
        out_specs=[pl.BlockSpec((tr, LANES), lambda i: (i, 0))] * 3,
        out_shape=[shp, shp, shp],
        compiler_params=_params(1),
        name="gate_prep",
    )(g, pad(b_i), pad(b_f))


def _mlstm_kernel(q_ref, k_ref, v_ref, mo_ref, b_ref, li_ref, r_ref, g_ref, o_ref,
                  cn_ref, m_ref, *, heads, dk, dv, chunk, n_chunks):
    @pl.when(pl.program_id(0) == 0)
    def _():
        cn_ref[...] = jnp.zeros_like(cn_ref)
        m_ref[...] = jnp.zeros_like(m_ref)

    row = lax.broadcasted_iota(jnp.int32, (chunk, chunk), 0)
    col = lax.broadcasted_iota(jnp.int32, (chunk, chunk), 1)
    tril = row >= col
    ones_ext = (lax.broadcasted_iota(jnp.int32, (chunk, LANES), 1) == 0).astype(BF16)

    def chunk_body(c, carry):
        r0 = pl.multiple_of(c * chunk, chunk)
        rows = pl.ds(r0, chunk)
        b_all = b_ref[rows, :]
        li_all = li_ref[rows, :]
        r_all = r_ref[c]
        for h in range(heads):
            q = q_ref[rows, h * dk:(h + 1) * dk]
            k = k_ref[rows, h * dk:(h + 1) * dk]
            v_ext = jnp.concatenate([v_ref[rows, h * dv:(h + 1) * dv], ones_ext], axis=1)
            b_col = b_all[:, h:h + 1]
            li_col = li_all[:, h:h + 1]
            r_row = r_all[h:h + 1, :]
            m_prev = m_ref[h:h + 1, 0:1]
            cn = cn_ref[h]

            log_d = jnp.where(tril, b_col + r_row, -jnp.inf)
            m_inter = b_col + m_prev
            m_t = jnp.maximum(m_inter, jnp.max(log_d, axis=-1, keepdims=True))
            s = lax.dot_general(q, k, (((1,), (1,)), ((), ())), preferred_element_type=F32)
            w = (s * jnp.exp(log_d - m_t)).astype(BF16)
            decay = jnp.exp(m_inter - m_t)
            tot = decay * _dot(q, cn.astype(BF16)) + _dot(w, v_ext)
            num = tot[:, :dv]
            den = tot[:, dv:dv + 1]
            hh = num / jnp.maximum(jnp.abs(den), jnp.exp(-m_t))

            ms = jnp.mean(hh * hh, axis=-1, keepdims=True)
            y = hh * lax.rsqrt(ms + EPS) * g_ref[:, h * dv:(h + 1) * dv]
            y = y * jax.nn.sigmoid(mo_ref[rows, h * dv:(h + 1) * dv])
            o_ref[rows, h * dv:(h + 1) * dv] = y.astype(o_ref.dtype)

            b_last = b_col[chunk - 1:chunk, :]
            a_col = b_last - b_col + li_col
            m_new = jnp.maximum(b_last + m_prev, jnp.max(a_col, axis=0, keepdims=True))
            src = jnp.exp(a_col - m_new)
            sv = (src * v_ext.astype(F32)).astype(BF16)
            upd = lax.dot_general(k, sv, (((0,), (0,)), ((), ())), preferred_element_type=F32)
            cn_ref[h] = jnp.exp(b_last + m_prev - m_new) * cn + upd
            m_ref[h:h + 1, :] = jnp.broadcast_to(m_new, (1, LANES))
        return carry

    lax.fori_loop(0, n_chunks, chunk_body, 0)


def mlstm(q, k, p16, v_blk, p32, mo_blk, b, li, r_rows, g_out, heads, dk, dv, chunk):
    s = q.shape[0]
    rows = _pick(s, (512, 256, 128, 64))
    n_chunks = rows // chunk
    wv = heads * dv
    return pl.pallas_call(
        functools.partial(_mlstm_kernel, heads=heads, dk=dk, dv=dv, chunk=chunk, n_chunks=n_chunks),
        grid=(s // rows,),
        in_specs=[pl.BlockSpec((rows, heads * dk), lambda i: (i, 0)),
                  pl.BlockSpec((rows, heads * dk), lambda i: (i, 0)),
                  pl.BlockSpec((rows, wv), lambda i: (i, v_blk)),
                  pl.BlockSpec((rows, wv), lambda i: (i, mo_blk)),
                  pl.BlockSpec((rows, LANES), lambda i: (i, 0)),
                  pl.BlockSpec((rows, LANES), lambda i: (i, 0)),
                  pl.BlockSpec((n_chunks, heads, chunk), lambda i: (i, 0, 0)),
                  pl.BlockSpec((1, wv), lambda i: (0, 0))],
        out_specs=pl.BlockSpec((rows, wv), lambda i: (i, 0)),
        out_shape=jax.ShapeDtypeStruct((s, wv), BF16),
        scratch_shapes=[pltpu.VMEM((heads, dk, dv + LANES), F32),
                        pltpu.VMEM((heads, LANES), F32)],
        compiler_params=_params(1),
        name="mlstm",
    )(q, k, p16, p32, b, li, r_rows, g_out.reshape(1, wv).astype(F32))


def _sb_kernel(q_ref, k_ref, v_ref, o_ref, acc_ref, run_ref, live_ref, *, tq, tk):
    i = pl.program_id(1)
    q = q_ref[...]
    acc_ref[...] = jnp.zeros_like(acc_ref)
    run_ref[...] = jnp.zeros_like(run_ref)
    live_ref[0] = 1
    n_blocks = (i + 1) * (tq // tk)
    suffix_excl = (lax.broadcasted_iota(jnp.int32, (tk, tk), 0)
                   > lax.broadcasted_iota(jnp.int32, (tk, tk), 1)).astype(BF16)
    q_pos = i * tq + lax.broadcasted_iota(jnp.int32, (tq, tk), 0)
    k_off = lax.broadcasted_iota(jnp.int32, (tq, tk), 1)

    def block_body(step, carry):
        @pl.when(live_ref[0] > 0)
        def _():
            k0 = pl.multiple_of((n_blocks - 1 - step) * tk, tk)
            kk = k_ref[pl.ds(k0, tk), :]
            vv = v_ref[pl.ds(k0, tk), :]
            z = lax.dot_general(q, kk, (((1,), (1,)), ((), ())), preferred_element_type=F32)
            log_beta = _log_sigmoid(z)
            strict = (k0 + k_off) < q_pos
            log_rest = jnp.where(strict, log_beta - z, 0.0)
            hi = log_rest.astype(BF16)
            lo = (log_rest - hi.astype(F32)).astype(BF16)
            after = _dot(hi, suffix_excl) + _dot(lo, suffix_excl)
            run = run_ref[...]
            wts = jnp.where(strict, jnp.exp(log_beta + after + run), 0.0)
            acc_ref[...] += _dot(wts.astype(BF16), vv)
            run = run + after[:, 0:1] + log_rest[:, 0:1]
            run_ref[...] = run
            live_ref[0] = (jnp.max(run) > SB_DONE_LOG).astype(jnp.int32)
        return carry

    lax.fori_loop(0, n_blocks, block_body, 0)
    o_ref[...] = acc_ref[...].astype(o_ref.dtype)


def stick_breaking(p16, q_blk0, k_blk0, v_blk0, heads, d):
    s = p16.shape[0]
    tq = _pick(s, (256, 128))
    tk = 128
    return pl.pallas_call(
        functools.partial(_sb_kernel, tq=tq, tk=tk),
        grid=(heads, s // tq),
        in_specs=[pl.BlockSpec((tq, d), lambda h, i: (i, q_blk0 + h)),
                  pl.BlockSpec((s, d), lambda h, i: (0, k_blk0 + h)),
                  pl.BlockSpec((s, d), lambda h, i: (0, v_blk0 + h))],
        out_specs=pl.BlockSpec((tq, d), lambda h, i: (i, h)),
        out_shape=jax.ShapeDtypeStruct((s, heads * d), BF16),
        scratch_shapes=[pltpu.VMEM((tq, d), F32),
                        pltpu.VMEM((tq, 1), F32),
                        pltpu.SMEM((1,), jnp.int32)],
        compiler_params=_params(2),
        name="stick_breaking",
    )(p16, p16, p16)


def _ffn(x, g, w1, w3, w2):
    h = rmsnorm(x, g, BF16)
    u = gateup(h, w1.astype(BF16), w3.astype(BF16))
    return matmul_residual(u, w2.astype(BF16), x, 0.5)


def _mixers(x, g_mix, w_in, conv_qk, b_igate, b_fgate, g_mlstm_out, w_proj_a, w_proj_b, w_out):
    d_model = x.shape[1]
    qk_w = ML_HEADS * ML_QK_DIM
    ml_w = ML_HEADS * ML_V_DIM
    sb_w = SB_HEADS * SB_HEAD_DIM
    o_mv = 2 * qk_w
    o_mo = o_mv + ml_w
    o_mi = o_mo + ml_w
    o_mf = o_mi + ML_HEADS
    o_sq = o_mf + ML_HEADS
    o_ga = o_sq + 3 * sb_w
    wb = w_in.astype(BF16)
    w32 = jnp.concatenate([wb[:, :o_mv], wb[:, o_mo:o_mi], wb[:, o_ga:]], axis=1)
    w16 = jnp.concatenate([wb[:, o_mv:o_mo], wb[:, o_sq:o_ga]], axis=1)
    zpad = jnp.zeros((d_model, LANES - ML_HEADS), BF16)
    wg = jnp.concatenate([wb[:, o_mi:o_mf], zpad, wb[:, o_mf:o_sq], zpad], axis=1)
    scale16 = jnp.concatenate([jnp.ones((ml_w,), F32),
                               jnp.full((sb_w,), SB_HEAD_DIM ** -0.5, F32),
                               jnp.ones((2 * sb_w,), F32)])

    h = rmsnorm(x, g_mix, BF16)
    p32 = project(h, w32, jnp.ones((w32.shape[1],), F32), F32)
    p16 = project(h, w16, scale16, BF16)
    pg = project(h, wg, jnp.ones((2 * LANES,), F32), F32)

    q, k = conv_silu_qk(p32, conv_qk, 2 * qk_w, ML_QK_DIM ** -0.5)
    b, li, r = gate_prep(pg, b_igate, b_fgate, ML_CHUNK)
    s = x.shape[0]
    r_rows = jnp.swapaxes(r[:, :ML_HEADS].reshape(s // ML_CHUNK, ML_CHUNK, ML_HEADS), 1, 2)
    ya = mlstm(q, k, p16, 0, p32, (2 * qk_w) // ml_w, b, li, r_rows, g_mlstm_out,
               ML_HEADS, ML_QK_DIM, ML_V_DIM, ML_CHUNK)

    blk = lambda col: col // SB_HEAD_DIM
    yb = stick_breaking(p16, blk(ml_w), blk(ml_w + sb_w), blk(ml_w + 2 * sb_w), SB_HEADS, SB_HEAD_DIM)

    merged = gated_merge(ya, yb, w_proj_a.astype(BF16), w_proj_b.astype(BF16), p32,
                         2 * qk_w + ml_w, 2 * qk_w + ml_w + d_model)
    return matmul_residual(merged, w_out.astype(BF16), x, 1.0)


def kernel(x, g_ffn1, w1_ffn1, w3_ffn1, w2_ffn1, g_mix, w_in, conv_qk, b_igate, b_fgate,
           g_mlstm_out, w_proj_a, w_proj_b, w_out, g_ffn2, w1_ffn2, w3_ffn2, w2_ffn2, g_final):
    batch, seq, d_model = x.shape
    outs = []
    for bi in range(batch):
        xb = x[bi]
        for l in range(g_ffn1.shape[0]):
            xb = _ffn(xb, g_ffn1[l], w1_ffn1[l], w3_ffn1[l], w2_ffn1[l])
            xb = _mixers(xb, g_mix[l], w_in[l], conv_qk[l], b_igate[l], b_fgate[l],
                         g_mlstm_out[l], w_proj_a[l], w_proj_b[l], w_out[l])
            xb = _ffn(xb, g_ffn2[l], w1_ffn2[l], w3_ffn2[l], w2_ffn2[l])
        outs.append(rmsnorm(xb, g_final, x.dtype))
    return jnp.stack(outs, axis=0)
```

```python
import functools

import jax
import jax.numpy as jnp
from jax import lax
from jax.experimental import pallas as pl
from jax.experimental.pallas import tpu as pltpu

F32 = jnp.float32
BF16 = jnp.bfloat16
EPS = 1e-6

V7X_VMEM_BYTES = 64 * 1024 * 1024
VMEM_LIMIT_BYTES = V7X_VMEM_BYTES - 8 * 1024 * 1024
LANES = 128
BF16_SUBLANES = 16

ML_HEADS = 8
ML_QK_DIM = 128
ML_V_DIM = 256
ML_CHUNK = 64
SB_HEADS = 16
SB_HEAD_DIM = 128
SB_HEADS_PER_STEP = 4
CONV_WIDTH = 4
SB_DONE_LOG = -100.0


def _params(n_grid):
    return pltpu.CompilerParams(
        dimension_semantics=("arbitrary",) * n_grid,
        vmem_limit_bytes=VMEM_LIMIT_BYTES)


def _pick(n, candidates):
    for c in candidates:
        if n % c == 0:
            return c
    raise ValueError(f"no tile in {candidates} divides {n}")


def _log_sigmoid(x):
    return jnp.minimum(x, 0.0) - jnp.log1p(jnp.exp(-jnp.abs(x)))


def _dot(a, b):
    return jnp.dot(a, b, preferred_element_type=F32)


def _dot_nt(a, b):
    return lax.dot_general(a, b, (((1,), (1,)), ((), ())), preferred_element_type=F32)


def _rmsnorm_kernel(x_ref, g_ref, o_ref):
    x = x_ref[...]
    ms = jnp.mean(x * x, axis=-1, keepdims=True)
    o_ref[...] = (x * lax.rsqrt(ms + EPS) * g_ref[...]).astype(o_ref.dtype)


def rmsnorm(x, g, out_dtype):
    s, d = x.shape
    tr = _pick(s, (256, 128, 64, 8))
    return pl.pallas_call(
        _rmsnorm_kernel,
        grid=(s // tr,),
        in_specs=[pl.BlockSpec((tr, d), lambda i: (i, 0)),
                  pl.BlockSpec((1, d), lambda i: (0, 0))],
        out_specs=pl.BlockSpec((tr, d), lambda i: (i, 0)),
        out_shape=jax.ShapeDtypeStruct((s, d), out_dtype),
        compiler_params=_params(1),
        name="rmsnorm",
    )(x, g.reshape(1, d).astype(F32))


def _gateup_kernel(a_ref, w1_ref, w3_ref, w2_ref, o_ref, w2b_ref):
    a = a_ref[...]
    g = _dot(a, w1_ref[...].astype(BF16))
    u = _dot(a, w3_ref[...].astype(BF16))
    o_ref[...] = (g * jax.nn.sigmoid(g) * u).astype(o_ref.dtype)
    w2b_ref[...] = w2_ref[...].astype(BF16)


def gateup(h, w1, w3, w2):
    m, k = h.shape
    n = w1.shape[1]
    k2, n2 = w2.shape
    tm = _pick(m, (1024, 512, 256, 128, 64, 8))
    tn = _pick(n, (256, 128))
    n_i, n_j = m // tm, n // tn
    slab = k2 // (n_i * n_j)
    assert slab * n_i * n_j == k2 and slab % BF16_SUBLANES == 0, (k2, n_i, n_j)
    return pl.pallas_call(
        _gateup_kernel,
        grid=(n_i, n_j),
        in_specs=[pl.BlockSpec((tm, k), lambda i, j: (i, 0)),
                  pl.BlockSpec((k, tn), lambda i, j: (0, j)),
                  pl.BlockSpec((k, tn), lambda i, j: (0, j)),
                  pl.BlockSpec((slab, n2), lambda i, j: (i * n_j + j, 0))],
        out_specs=[pl.BlockSpec((tm, tn), lambda i, j: (i, j)),
                   pl.BlockSpec((slab, n2), lambda i, j: (i * n_j + j, 0))],
        out_shape=[jax.ShapeDtypeStruct((m, n), BF16),
                   jax.ShapeDtypeStruct((k2, n2), BF16)],
        compiler_params=_params(2),
        name="ffn_gateup",
    )(h, w1, w3, w2)


def _resid_kernel(a_ref, w_ref, x_ref, o_ref, *, alpha):
    o_ref[...] = x_ref[...] + alpha * _dot(a_ref[...], w_ref[...].astype(BF16))


def matmul_residual(a, w, x, alpha, tm_max):
    m, k = a.shape
    n = w.shape[1]
    tm = _pick(m, tuple(t for t in (1024, 512, 256, 128, 64, 8) if t <= tm_max))
    tn = _pick(n, (256, 128))
    return pl.pallas_call(
        functools.partial(_resid_kernel, alpha=alpha),
        grid=(m // tm, n // tn),
        in_specs=[pl.BlockSpec((tm, k), lambda i, j: (i, 0)),
                  pl.BlockSpec((k, tn), lambda i, j: (0, j)),
                  pl.BlockSpec((tm, tn), lambda i, j: (i, j))],
        out_specs=pl.BlockSpec((tm, tn), lambda i, j: (i, j)),
        out_shape=jax.ShapeDtypeStruct((m, n), F32),
        compiler_params=_params(2),
        name="matmul_residual",
    )(a, w, x)


def _proj_kernel(a_ref, w_ref, o_ref):
    o_ref[...] = _dot(a_ref[...], w_ref[...].astype(BF16)).astype(o_ref.dtype)


def _proj_scaled_kernel(a_ref, w_ref, s_ref, o_ref):
    o_ref[...] = (_dot(a_ref[...], w_ref[...].astype(BF16)) * s_ref[...]).astype(o_ref.dtype)


def project(a, w, col0, n, out_dtype, col_scale=None):
    m, k = a.shape
    tm = _pick(m, (1024, 512, 256, 128, 64, 8))
    tn = _pick(n, (256, 128))
    assert col0 % tn == 0
    blk0 = col0 // tn
    in_specs = [pl.BlockSpec((tm, k), lambda i, j: (i, 0)),
                pl.BlockSpec((k, tn), lambda i, j: (0, blk0 + j))]
    args = [a, w]
    body = _proj_kernel
    if col_scale is not None:
        in_specs.append(pl.BlockSpec((1, tn), lambda i, j: (0, j)))
        args.append(col_scale.reshape(1, n).astype(F32))
        body = _proj_scaled_kernel
    return pl.pallas_call(
        body,
        grid=(m // tm, n // tn),
        in_specs=in_specs,
        out_specs=pl.BlockSpec((tm, tn), lambda i, j: (i, j)),
        out_shape=jax.ShapeDtypeStruct((m, n), out_dtype),
        compiler_params=_params(2),
        name="in_project",
    )(*args)


def _merge_kernel(ya_ref, yb_ref, wa_ref, wb_ref, ga_ref, gb_ref, o_ref):
    pa = _dot(ya_ref[...], wa_ref[...].astype(BF16))
    pb = _dot(yb_ref[...], wb_ref[...].astype(BF16))
    o_ref[...] = (jax.nn.sigmoid(ga_ref[...]) * pa
                  + jax.nn.sigmoid(gb_ref[...]) * pb).astype(o_ref.dtype)


def gated_merge(ya, yb, wa, wb, gates):
    m, k = ya.shape
    n = wa.shape[1]
    tm = _pick(m, (1024, 512, 256, 128, 64, 8))
    tn = _pick(n, (256, 128))
    gb_blk = n // tn
    return pl.pallas_call(
        _merge_kernel,
        grid=(m // tm, n // tn),
        in_specs=[pl.BlockSpec((tm, k), lambda i, j: (i, 0)),
                  pl.BlockSpec((tm, k), lambda i, j: (i, 0)),
                  pl.BlockSpec((k, tn), lambda i, j: (0, j)),
                  pl.BlockSpec((k, tn), lambda i, j: (0, j)),
                  pl.BlockSpec((tm, tn), lambda i, j: (i, j)),
                  pl.BlockSpec((tm, tn), lambda i, j: (i, gb_blk + j))],
        out_specs=pl.BlockSpec((tm, tn), lambda i, j: (i, j)),
        out_shape=jax.ShapeDtypeStruct((m, n), BF16),
        compiler_params=_params(2),
        name="gated_merge",
    )(ya, yb, wa, wb, gates, gates)


def _conv_kernel(cur_ref, prev_ref, w_ref, q_ref, k_ref, *, k_scale):
    cur = cur_ref[...]
    prev = jnp.where(pl.program_id(0) > 0, prev_ref[...], 0.0)
    ext = jnp.concatenate([prev, cur], axis=0)
    w = w_ref[...]
    tr = cur.shape[0]
    acc = w[CONV_WIDTH - 1:CONV_WIDTH] * cur
    for j in range(CONV_WIDTH - 1):
        off = 8 - (CONV_WIDTH - 1) + j
        acc = acc + w[j:j + 1] * ext[off:off + tr]
    y = acc * jax.nn.sigmoid(acc)
    half = y.shape[1] // 2
    q_ref[...] = y[:, :half].astype(q_ref.dtype)
    k_ref[...] = (y[:, half:] * k_scale).astype(k_ref.dtype)


def conv_silu_qk(p, conv_w, k_scale):
    s, width2 = p.shape
    tr = _pick(s, (256, 128, 64, 8))
    half = width2 // 2
    return pl.pallas_call(
        functools.partial(_conv_kernel, k_scale=k_scale),
        grid=(s // tr,),
        in_specs=[pl.BlockSpec((tr, width2), lambda i: (i, 0)),
                  pl.BlockSpec((8, width2), lambda i: (jnp.maximum(i * (tr // 8) - 1, 0), 0)),
                  pl.BlockSpec((CONV_WIDTH, width2), lambda i: (0, 0))],
        out_specs=[pl.BlockSpec((tr, half), lambda i: (i, 0)),
                   pl.BlockSpec((tr, half), lambda i: (i, 0))],
        out_shape=[jax.ShapeDtypeStruct((s, half), BF16),
                   jax.ShapeDtypeStruct((s, half), BF16)],
        compiler_params=_params(1),
        name="conv_silu_qk",
    )(p, p, conv_w.astype(F32))


def _gate_kernel(g_ref, bi_ref, bf_ref, b_ref, li_ref, r_ref, *, chunk):
    g = g_ref[...]
    li = g[:, :LANES] + bi_ref[...]
    lf = _log_sigmoid(g[:, LANES:] + bf_ref[...])
    pos = lax.broadcasted_iota(jnp.int32, lf.shape, 0) % chunk
    b = lf
    d = 1
    while d < chunk:
        b = b + jnp.where(pos >= d, pltpu.roll(b, d, 0), 0.0)
        d *= 2
    b_ref[...] = b
    li_ref[...] = li
    r_ref[...] = li - b


def gate_prep(g, b_i, b_f, chunk):
    s = g.shape[0]
    tr = _pick(s, (512, 256, 128, 64))
    pad = lambda v: jnp.zeros((1, LANES), F32).at[0, :v.shape[0]].set(v.astype(F32))
    shp = jax.ShapeDtypeStruct((s, LANES), F32)
    return pl.pallas_call(
        functools.partial(_gate_kernel, chunk=chunk),
        grid=(s // tr,),
        in_specs=[pl.BlockSpec((tr, 2 * LANES), lambda i: (i, 0)),
                  pl.BlockSpec((1, LANES), lambda i: (0, 0)),
                  pl.BlockSpec((1, LANES), lambda i: (0, 0))],
        out_specs=[pl.BlockSpec((tr, LANES), lambda i: (i, 0))] * 3,
        out_shape=[shp, shp, shp],
        compiler_params=_params(1),
        name="gate_prep",
    )(g, pad(b_i), pad(b_f))


def _mlstm_kernel(q_ref, k_ref, v_ref, mo_ref, b_ref, li_ref, r_ref, g_ref, o_ref,
                  cn_ref, m_ref, *, heads, dk, dv, chunk, n_chunks):
    @pl.when(pl.program_id(0) == 0)
    def _():
        cn_ref[...] = jnp.zeros_like(cn_ref)
        m_ref[...] = jnp.zeros_like(m_ref)

    row = lax.broadcasted_iota(jnp.int32, (chunk, chunk), 0)
    col = lax.broadcasted_iota(jnp.int32, (chunk, chunk), 1)
    tril = row >= col
    ones_ext = (lax.broadcasted_iota(jnp.int32, (chunk, LANES), 1) == 0).astype(BF16)

    def chunk_body(c, carry):
        r0 = pl.multiple_of(c * chunk, chunk)
        rows = pl.ds(r0, chunk)
        b_all = b_ref[rows, :]
        li_all = li_ref[rows, :]
        r_all = r_ref[c]
        for h in range(heads):
            q = q_ref[rows, h * dk:(h + 1) * dk]
            k = k_ref[rows, h * dk:(h + 1) * dk]
            v_ext = jnp.concatenate([v_ref[rows, h * dv:(h + 1) * dv], ones_ext], axis=1)
            b_col = b_all[:, h:h + 1]
            li_col = li_all[:, h:h + 1]
            r_row = r_all[h:h + 1, :]
            m_prev = m_ref[h:h + 1, 0:1]
            cn = cn_ref[h]

            log_d = jnp.where(tril, b_col + r_row, -jnp.inf)
            m_inter = b_col + m_prev
            m_t = jnp.maximum(m_inter, jnp.max(log_d, axis=-1, keepdims=True))
            s = _dot_nt(q, k)
            w = (s * jnp.exp(log_d - m_t)).astype(BF16)
            decay = jnp.exp(m_inter - m_t)
            tot = decay * _dot(q, cn.astype(BF16)) + _dot(w, v_ext)
            num = tot[:, :dv]
            den = tot[:, dv:dv + 1]
            hh = num / jnp.maximum(jnp.abs(den), jnp.exp(-m_t))

            ms = jnp.mean(hh * hh, axis=-1, keepdims=True)
            y = hh * lax.rsqrt(ms + EPS) * g_ref[:, h * dv:(h + 1) * dv]
            y = y * jax.nn.sigmoid(mo_ref[rows, h * dv:(h + 1) * dv])
            o_ref[rows, h * dv:(h + 1) * dv] = y.astype(o_ref.dtype)

            b_last = b_col[chunk - 1:chunk, :]
            a_col = b_last - b_col + li_col
            m_new = jnp.maximum(b_last + m_prev, jnp.max(a_col, axis=0, keepdims=True))
            src = jnp.exp(a_col - m_new)
            sv = (src * v_ext.astype(F32)).astype(BF16)
            upd = lax.dot_general(k, sv, (((0,), (0,)), ((), ())), preferred_element_type=F32)
            cn_ref[h] = jnp.exp(b_last + m_prev - m_new) * cn + upd
            m_ref[h:h + 1, :] = jnp.broadcast_to(m_new, (1, LANES))
        return carry

    lax.fori_loop(0, n_chunks, chunk_body, 0)


def mlstm(q, k, v, mo, b, li, r_rows, g_out, heads, dk, dv, chunk):
    s = q.shape[0]
    rows = _pick(s, (512, 256, 128, 64))
    n_chunks = rows // chunk
    wv = heads * dv
    return pl.pallas_call(
        functools.partial(_mlstm_kernel, heads=heads, dk=dk, dv=dv, chunk=chunk, n_chunks=n_chunks),
        grid=(s // rows,),
        in_specs=[pl.BlockSpec((rows, heads * dk), lambda i: (i, 0)),
                  pl.BlockSpec((rows, heads * dk), lambda i: (i, 0)),
                  pl.BlockSpec((rows, wv), lambda i: (i, 0)),
                  pl.BlockSpec((rows, wv), lambda i: (i, 0)),
                  pl.BlockSpec((rows, LANES), lambda i: (i, 0)),
                  pl.BlockSpec((rows, LANES), lambda i: (i, 0)),
                  pl.BlockSpec((n_chunks, heads, chunk), lambda i: (i, 0, 0)),
                  pl.BlockSpec((1, wv), lambda i: (0, 0))],
        out_specs=pl.BlockSpec((rows, wv), lambda i: (i, 0)),
        out_shape=jax.ShapeDtypeStruct((s, wv), BF16),
        scratch_shapes=[pltpu.VMEM((heads, dk, dv + LANES), F32),
                        pltpu.VMEM((heads, LANES), F32)],
        compiler_params=_params(1),
        name="mlstm",
    )(q, k, v, mo, b, li, r_rows, g_out.reshape(1, wv).astype(F32))


def _sb_kernel(q_ref, k_ref, v_ref, o_ref, acc_ref, run_ref, live_ref, *, tq, tk, hp, d):
    i = pl.program_id(1)
    groups = tq // tk
    row = lax.broadcasted_iota(jnp.int32, (tk, tk), 0)
    col = lax.broadcasted_iota(jnp.int32, (tk, tk), 1)
    diag_strict = col < row
    sums_rhs = jnp.concatenate([(row > col).astype(BF16), jnp.ones((tk, tk), BF16)], axis=1)
    sums_rhs = jnp.concatenate([sums_rhs, sums_rhs], axis=0)

    def walk(tiles):
        zs = [_dot_nt(q, kk) for q, kk, _, _, _ in tiles]
        log_betas, splits = [], []
        for z, (_, _, _, _, mask) in zip(zs, tiles):
            log_beta = _log_sigmoid(z)
            log_rest = log_beta - z
            if mask is not None:
                log_rest = jnp.where(mask, log_rest, 0.0)
            hi = log_rest.astype(BF16)
            lo = (log_rest - hi.astype(F32)).astype(BF16)
            log_betas.append(log_beta)
            splits.append(jnp.concatenate([hi, lo], axis=1))
        sums = [_dot(x, sums_rhs) for x in splits]
        weights, runs = [], []
        for log_beta, sm, (_, _, _, run, mask) in zip(log_betas, sums, tiles):
            log_w = log_beta + sm[:, :tk]
            wts = jnp.exp(log_w if run is None else log_w + run)
            if mask is not None:
                wts = jnp.where(mask, wts, 0.0)
            weights.append(wts.astype(BF16))
            runs.append(sm[:, tk:] if run is None else run + sm[:, tk:])
        outs = [_dot(w, vv) for w, (_, _, vv, _, _) in zip(weights, tiles)]
        return list(zip(outs, runs))

    def key_rows(blk):
        return pl.ds(pl.multiple_of(blk * tk, tk), tk)

    def cols(h):
        return slice(h * d, (h + 1) * d)

    def rows(g):
        return slice(g * tk, (g + 1) * tk)

    acc = {}
    run = {}
    for r in range(groups):
        ids = [(h, g) for h in range(hp) for g in range(r, groups)]
        tiles = []
        for h, g in ids:
            kr = key_rows(i * groups + g - r)
            tiles.append((q_ref[rows(g), cols(h)], k_ref[kr, cols(h)], v_ref[kr, cols(h)],
                          run.get((h, g)), diag_strict if r == 0 else None))
        for key, (out, new_run) in zip(ids, walk(tiles)):
            acc[key] = out if r == 0 else acc[key] + out
            run[key] = new_run
    for (h, g), value in acc.items():
        acc_ref[h, rows(g), :] = value
        run_ref[h, rows(g), :] = run[(h, g)]
    live_ref[0] = (jnp.max(run_ref[...]) > SB_DONE_LOG).astype(jnp.int32)

    n_prev = i * groups

    def block_body(step, carry):
        @pl.when(live_ref[0] > 0)
        def _():
            kr = key_rows(n_prev - 1 - step)
            tiles = [(q_ref[:, cols(h)], k_ref[kr, cols(h)], v_ref[kr, cols(h)], run_ref[h], None)
                     for h in range(hp)]
            for h, (out, new_run) in enumerate(walk(tiles)):
                acc_ref[h] += out
                run_ref[h] = new_run
            live_ref[0] = (jnp.max(run_ref[...]) > SB_DONE_LOG).astype(jnp.int32)
        return carry

    lax.fori_loop(0, n_prev, block_body, 0)
    for h in range(hp):
        o_ref[:, cols(h)] = acc_ref[h].astype(o_ref.dtype)


def stick_breaking(qkv, heads, d):
    s = qkv.shape[0]
    tq = _pick(s, (256, 128))
    tk = 128
    hp = SB_HEADS_PER_STEP
    assert heads % hp == 0
    n_hg = heads // hp
    return pl.pallas_call(
        functools.partial(_sb_kernel, tq=tq, tk=tk, hp=hp, d=d),
        grid=(n_hg, s // tq),
        in_specs=[pl.BlockSpec((tq, hp * d), lambda hg, i: (i, hg)),
                  pl.BlockSpec((s, hp * d), lambda hg, i: (0, n_hg + hg)),
                  pl.BlockSpec((s, hp * d), lambda hg, i: (0, 2 * n_hg + hg))],
        out_specs=pl.BlockSpec((tq, hp * d), lambda hg, i: (i, hg)),
        out_shape=jax.ShapeDtypeStruct((s, heads * d), BF16),
        scratch_shapes=[pltpu.VMEM((hp, tq, d), F32),
                        pltpu.VMEM((hp, tq, tk), F32),
                        pltpu.SMEM((1,), jnp.int32)],
        compiler_params=_params(2),
        name="stick_breaking",
    )(qkv, qkv, qkv)


def _ffn(x, g, w1, w3, w2):
    h = rmsnorm(x, g, BF16)
    u, w2b = gateup(h, w1, w3, w2)
    return matmul_residual(u, w2b, x, 0.5, tm_max=512)


def _mixers(x, g_mix, w_in, conv_qk, b_igate, b_fgate, g_mlstm_out, w_proj_a, w_proj_b, w_out):
    d_model = x.shape[1]
    qk_w = ML_HEADS * ML_QK_DIM
    ml_w = ML_HEADS * ML_V_DIM
    sb_w = SB_HEADS * SB_HEAD_DIM
    o_mv = 2 * qk_w
    o_mo = o_mv + ml_w
    o_mi = o_mo + ml_w
    o_mf = o_mi + ML_HEADS
    o_sq = o_mf + ML_HEADS
    w_tail = w_in[:, o_sq:].astype(BF16)
    zpad = jnp.zeros((d_model, LANES - ML_HEADS), BF16)
    w_gate = jnp.concatenate([w_in[:, o_mi:o_mf].astype(BF16), zpad,
                              w_in[:, o_mf:o_sq].astype(BF16), zpad], axis=1)
    q_scale = jnp.concatenate([jnp.full((sb_w,), SB_HEAD_DIM ** -0.5, F32),
                               jnp.ones((2 * sb_w,), F32)])

    h = rmsnorm(x, g_mix, BF16)
    qk_pre = project(h, w_in, 0, o_mv, F32)
    mv = project(h, w_in, o_mv, ml_w, BF16)
    mo = project(h, w_in, o_mo, ml_w, F32)
    pg = project(h, w_gate, 0, 2 * LANES, F32)
    s_qkv = project(h, w_tail, 0, 3 * sb_w, BF16, col_scale=q_scale)
    gates = project(h, w_tail, 3 * sb_w, 2 * d_model, F32)

    q, k = conv_silu_qk(qk_pre, conv_qk, ML_QK_DIM ** -0.5)
    b, li, r = gate_prep(pg, b_igate, b_fgate, ML_CHUNK)
    s = x.shape[0]
    r_rows = jnp.swapaxes(r[:, :ML_HEADS].reshape(s // ML_CHUNK, ML_CHUNK, ML_HEADS), 1, 2)
    ya = mlstm(q, k, mv, mo, b, li, r_rows, g_mlstm_out, ML_HEADS, ML_QK_DIM, ML_V_DIM, ML_CHUNK)

    yb = stick_breaking(s_qkv, SB_HEADS, SB_HEAD_DIM)

    merged = gated_merge(ya, yb, w_proj_a, w_proj_b, gates)
    return matmul_residual(merged, w_out, x, 1.0, tm_max=1024)


def kernel(x, g_ffn1, w1_ffn1, w3_ffn1, w2_ffn1, g_mix, w_in, conv_qk, b_igate, b_fgate,
           g_mlstm_out, w_proj_a, w_proj_b, w_out, g_ffn2, w1_ffn2, w3_ffn2, w2_ffn2, g_final):
    batch, seq, d_model = x.shape
    outs = []
    for bi in range(batch):
        xb = x[bi]
        for l in range(g_ffn1.shape[0]):
            xb = _ffn(xb, g_ffn1[l], w1_ffn1[l], w3_ffn1[l], w2_ffn1[l])
            xb = _mixers(xb, g_mix[l], w_in[l], conv_qk[l], b_igate[l], b_fgate[l],
                         g_mlstm_out[l], w_proj_a[l], w_proj_b[l], w_out[l])
            xb = _ffn(xb, g_ffn2[l], w1_ffn2[l], w3_ffn2[l], w2_ffn2[l])
        outs.append(rmsnorm(xb, g_final, x.dtype))
    return jnp.stack(outs, axis=0)
```

```python
import functools

import jax
import jax.numpy as jnp
from jax import lax
from jax.experimental import pallas as pl
from jax.experimental.pallas import tpu as pltpu

F32 = jnp.float32
BF16 = jnp.bfloat16
EPS = 1e-6

V7X_VMEM_BYTES = 64 * 1024 * 1024
VMEM_LIMIT_BYTES = V7X_VMEM_BYTES - 8 * 1024 * 1024
LANES = 128
BF16_SUBLANES = 16

ML_HEADS = 8
ML_QK_DIM = 128
ML_V_DIM = 256
ML_CHUNK = 64
SB_HEADS = 16
SB_HEAD_DIM = 128
SB_HEADS_PER_STEP = 4
CONV_WIDTH = 4
SB_DONE_LOG = -100.0


def _params(n_grid):
    return pltpu.CompilerParams(
        dimension_semantics=("arbitrary",) * n_grid,
        vmem_limit_bytes=VMEM_LIMIT_BYTES)


def _pick(n, candidates):
    for c in candidates:
        if n % c == 0:
            return c
    raise ValueError(f"no tile in {candidates} divides {n}")


def _log_sigmoid(x):
    return jnp.minimum(x, 0.0) - jnp.log1p(jnp.exp(-jnp.abs(x)))


def _dot(a, b):
    return jnp.dot(a, b, preferred_element_type=F32)


def _dot_nt(a, b):
    return lax.dot_general(a, b, (((1,), (1,)), ((), ())), preferred_element_type=F32)


def _rmsnorm_kernel(x_ref, g_ref, o_ref):
    x = x_ref[...]
    ms = jnp.mean(x * x, axis=-1, keepdims=True)
    o_ref[...] = (x * lax.rsqrt(ms + EPS) * g_ref[...]).astype(o_ref.dtype)


def rmsnorm(x, g, out_dtype):
    s, d = x.shape
    tr = _pick(s, (256, 128, 64, 8))
    return pl.pallas_call(
        _rmsnorm_kernel,
        grid=(s // tr,),
        in_specs=[pl.BlockSpec((tr, d), lambda i: (i, 0)),
                  pl.BlockSpec((1, d), lambda i: (0, 0))],
        out_specs=pl.BlockSpec((tr, d), lambda i: (i, 0)),
        out_shape=jax.ShapeDtypeStruct((s, d), out_dtype),
        compiler_params=_params(1),
        name="rmsnorm",
    )(x, g.reshape(1, d).astype(F32))


def _gateup_kernel(a_ref, w1_ref, w3_ref, w2_ref, o_ref, w2b_ref):
    a = a_ref[...]
    g = _dot(a, w1_ref[...].astype(BF16))
    u = _dot(a, w3_ref[...].astype(BF16))
    o_ref[...] = (g * jax.nn.sigmoid(g) * u).astype(o_ref.dtype)
    w2b_ref[...] = w2_ref[...].astype(BF16)


def gateup(h, w1, w3, w2):
    m, k = h.shape
    n = w1.shape[1]
    k2, n2 = w2.shape
    tm = _pick(m, (1024, 512, 256, 128, 64, 8))
    tn = _pick(n, (256, 128))
    n_i, n_j = m // tm, n // tn
    slab = k2 // (n_i * n_j)
    assert slab * n_i * n_j == k2 and slab % BF16_SUBLANES == 0, (k2, n_i, n_j)
    return pl.pallas_call(
        _gateup_kernel,
        grid=(n_i, n_j),
        in_specs=[pl.BlockSpec((tm, k), lambda i, j: (i, 0)),
                  pl.BlockSpec((k, tn), lambda i, j: (0, j)),
                  pl.BlockSpec((k, tn), lambda i, j: (0, j)),
                  pl.BlockSpec((slab, n2), lambda i, j: (i * n_j + j, 0))],
        out_specs=[pl.BlockSpec((tm, tn), lambda i, j: (i, j)),
                   pl.BlockSpec((slab, n2), lambda i, j: (i * n_j + j, 0))],
        out_shape=[jax.ShapeDtypeStruct((m, n), BF16),
                   jax.ShapeDtypeStruct((k2, n2), BF16)],
        compiler_params=_params(2),
        name="ffn_gateup",
    )(h, w1, w3, w2)


def _resid_kernel(a_ref, w_ref, x_ref, o_ref, *, alpha):
    o_ref[...] = x_ref[...] + alpha * _dot(a_ref[...], w_ref[...].astype(BF16))


def matmul_residual(a, w, x, alpha, tm_max, tn_max):
    m, k = a.shape
    n = w.shape[1]
    tm = _pick(m, tuple(t for t in (1024, 512, 256, 128, 64, 8) if t <= tm_max))
    tn = _pick(n, tuple(t for t in (512, 256, 128) if t <= tn_max))
    return pl.pallas_call(
        functools.partial(_resid_kernel, alpha=alpha),
        grid=(m // tm, n // tn),
        in_specs=[pl.BlockSpec((tm, k), lambda i, j: (i, 0)),
                  pl.BlockSpec((k, tn), lambda i, j: (0, j)),
                  pl.BlockSpec((tm, tn), lambda i, j: (i, j))],
        out_specs=pl.BlockSpec((tm, tn), lambda i, j: (i, j)),
        out_shape=jax.ShapeDtypeStruct((m, n), F32),
        compiler_params=_params(2),
        name="matmul_residual",
    )(a, w, x)


def _proj_kernel(a_ref, wt_ref, o_ref):
    o_ref[...] = _dot_nt(a_ref[...], wt_ref[...].astype(BF16)).astype(o_ref.dtype)


def _proj_scaled_kernel(a_ref, wt_ref, s_ref, o_ref):
    o_ref[...] = (_dot_nt(a_ref[...], wt_ref[...].astype(BF16)) * s_ref[...]).astype(o_ref.dtype)


def project(a, wt, segments, out_dtype, col_scale=None):
    m, k = a.shape
    n = sum(seg_n for _, seg_n in segments)
    tm = _pick(m, (1024, 512, 256, 128, 64, 8))
    tn = 512 if all(seg_n % 512 == 0 for _, seg_n in segments) else 256
    assert all(seg_n % tn == 0 and row0 % 8 == 0 for row0, seg_n in segments)

    def wt_row(j):
        blk0 = 0
        row8 = None
        for row0, seg_n in segments:
            here = row0 // 8 + (j - blk0) * (tn // 8)
            row8 = here if row8 is None else jnp.where(j >= blk0, here, row8)
            blk0 += seg_n // tn
        return row8 * 8

    in_specs = [pl.BlockSpec((tm, k), lambda i, j: (i, 0)),
                pl.BlockSpec((pl.Element(tn), pl.Element(k)), lambda i, j: (wt_row(j), 0))]
    args = [a, wt]
    body = _proj_kernel
    if col_scale is not None:
        in_specs.append(pl.BlockSpec((1, tn), lambda i, j: (0, j)))
        args.append(col_scale.reshape(1, n).astype(F32))
        body = _proj_scaled_kernel
    return pl.pallas_call(
        body,
        grid=(m // tm, n // tn),
        in_specs=in_specs,
        out_specs=pl.BlockSpec((tm, tn), lambda i, j: (i, j)),
        out_shape=jax.ShapeDtypeStruct((m, n), out_dtype),
        compiler_params=_params(2),
        name="in_project",
    )(*args)


def _merge_kernel(ya_ref, yb_ref, wa_ref, wb_ref, ga_ref, gb_ref, o_ref):
    pa = _dot(ya_ref[...], wa_ref[...].astype(BF16))
    pb = _dot(yb_ref[...], wb_ref[...].astype(BF16))
    o_ref[...] = (jax.nn.sigmoid(ga_ref[...]) * pa
                  + jax.nn.sigmoid(gb_ref[...]) * pb).astype(o_ref.dtype)


def gated_merge(ya, yb, wa, wb, gates, ga_col0, gb_col0):
    m, k = ya.shape
    n = wa.shape[1]
    tm = _pick(m, (1024, 512, 256, 128, 64, 8))
    tn = _pick(n, (512, 256, 128))
    assert ga_col0 % tn == 0 and gb_col0 % tn == 0
    ga_blk, gb_blk = ga_col0 // tn, gb_col0 // tn
    return pl.pallas_call(
        _merge_kernel,
        grid=(m // tm, n // tn),
        in_specs=[pl.BlockSpec((tm, k), lambda i, j: (i, 0)),
                  pl.BlockSpec((tm, k), lambda i, j: (i, 0)),
                  pl.BlockSpec((k, tn), lambda i, j: (0, j)),
                  pl.BlockSpec((k, tn), lambda i, j: (0, j)),
                  pl.BlockSpec((tm, tn), lambda i, j: (i, ga_blk + j)),
                  pl.BlockSpec((tm, tn), lambda i, j: (i, gb_blk + j))],
        out_specs=pl.BlockSpec((tm, tn), lambda i, j: (i, j)),
        out_shape=jax.ShapeDtypeStruct((m, n), BF16),
        compiler_params=_params(2),
        name="gated_merge",
    )(ya, yb, wa, wb, gates, gates)


def _conv_kernel(cur_ref, prev_ref, w_ref, q_ref, k_ref, *, k_scale):
    cur = cur_ref[...]
    prev = jnp.where(pl.program_id(0) > 0, prev_ref[...], 0.0)
    ext = jnp.concatenate([prev, cur], axis=0)
    w = w_ref[...]
    tr = cur.shape[0]
    acc = w[CONV_WIDTH - 1:CONV_WIDTH] * cur
    for j in range(CONV_WIDTH - 1):
        off = 8 - (CONV_WIDTH - 1) + j
        acc = acc + w[j:j + 1] * ext[off:off + tr]
    y = acc * jax.nn.sigmoid(acc)
    half = y.shape[1] // 2
    q_ref[...] = y[:, :half].astype(q_ref.dtype)
    k_ref[...] = (y[:, half:] * k_scale).astype(k_ref.dtype)


def conv_silu_qk(p, width2, conv_w, k_scale):
    s = p.shape[0]
    tr = _pick(s, (256, 128, 64, 8))
    half = width2 // 2
    return pl.pallas_call(
        functools.partial(_conv_kernel, k_scale=k_scale),
        grid=(s // tr,),
        in_specs=[pl.BlockSpec((tr, width2), lambda i: (i, 0)),
                  pl.BlockSpec((8, width2), lambda i: (jnp.maximum(i * (tr // 8) - 1, 0), 0)),
                  pl.BlockSpec((CONV_WIDTH, width2), lambda i: (0, 0))],
        out_specs=[pl.BlockSpec((tr, half), lambda i: (i, 0)),
                   pl.BlockSpec((tr, half), lambda i: (i, 0))],
        out_shape=[jax.ShapeDtypeStruct((s, half), BF16),
                   jax.ShapeDtypeStruct((s, half), BF16)],
        compiler_params=_params(1),
        name="conv_silu_qk",
    )(p, p, conv_w.astype(F32))


def _gate_kernel(g_ref, bi_ref, bf_ref, b_ref, li_ref, r_ref, *, chunk):
    g = g_ref[...]
    li = g[:, :LANES] + bi_ref[...]
    lf = _log_sigmoid(g[:, LANES:] + bf_ref[...])
    pos = lax.broadcasted_iota(jnp.int32, lf.shape, 0) % chunk
    b = lf
    d = 1
    while d < chunk:
        b = b + jnp.where(pos >= d, pltpu.roll(b, d, 0), 0.0)
        d *= 2
    b_ref[...] = b
    li_ref[...] = li
    r_ref[...] = li - b


def gate_prep(g, b_i, b_f, chunk):
    s = g.shape[0]
    tr = _pick(s, (512, 256, 128, 64))
    pad = lambda v: jnp.zeros((1, LANES), F32).at[0, :v.shape[0]].set(v.astype(F32))
    shp = jax.ShapeDtypeStruct((s, LANES), F32)
    return pl.pallas_call(
        functools.partial(_gate_kernel, chunk=chunk),
        grid=(s // tr,),
        in_specs=[pl.BlockSpec((tr, 2 * LANES), lambda i: (i, 0)),
                  pl.BlockSpec((1, LANES), lambda i: (0, 0)),
                  pl.BlockSpec((1, LANES), lambda i: (0, 0))],
        out_specs=[pl.BlockSpec((tr, LANES), lambda i: (i, 0))] * 3,
        out_shape=[shp, shp, shp],
        compiler_params=_params(1),
        name="gate_prep",
    )(g, pad(b_i), pad(b_f))


def _mlstm_kernel(q_ref, k_ref, v_ref, mo_ref, b_ref, li_ref, r_ref, g_ref, o_ref,
                  cn_ref, m_ref, *, heads, dk, dv, chunk, n_chunks):
    @pl.when(pl.program_id(0) == 0)
    def _():
        cn_ref[...] = jnp.zeros_like(cn_ref)
        m_ref[...] = jnp.zeros_like(m_ref)

    row = lax.broadcasted_iota(jnp.int32, (chunk, chunk), 0)
    col = lax.broadcasted_iota(jnp.int32, (chunk, chunk), 1)
    tril = row >= col
    ones_ext = (lax.broadcasted_iota(jnp.int32, (chunk, LANES), 1) == 0).astype(BF16)

    def chunk_body(c, carry):
        r0 = pl.multiple_of(c * chunk, chunk)
        rows = pl.ds(r0, chunk)
        b_all = b_ref[rows, :]
        li_all = li_ref[rows, :]
        r_all = r_ref[c]
        hs = range(heads)
        q = [q_ref[rows, h * dk:(h + 1) * dk] for h in hs]
        k = [k_ref[rows, h * dk:(h + 1) * dk] for h in hs]
        v_ext = [jnp.concatenate([v_ref[rows, h * dv:(h + 1) * dv], ones_ext], axis=1) for h in hs]
        cn = [cn_ref[h] for h in hs]
        s = [_dot_nt(q[h], k[h]) for h in hs]
        inter = [_dot(q[h], cn[h].astype(BF16)) for h in hs]

        b_col = [b_all[:, h:h + 1] for h in hs]
        m_prev = [m_ref[h:h + 1, 0:1] for h in hs]
        w, m_t, decay = [], [], []
        for h in hs:
            log_d = jnp.where(tril, b_col[h] + r_all[h:h + 1, :], -jnp.inf)
            m_inter = b_col[h] + m_prev[h]
            m_t.append(jnp.maximum(m_inter, jnp.max(log_d, axis=-1, keepdims=True)))
            w.append((s[h] * jnp.exp(log_d - m_t[h])).astype(BF16))
            decay.append(jnp.exp(m_inter - m_t[h]))
        intra = [_dot(w[h], v_ext[h]) for h in hs]

        for h in hs:
            tot = decay[h] * inter[h] + intra[h]
            hh = tot[:, :dv] / jnp.maximum(jnp.abs(tot[:, dv:dv + 1]), jnp.exp(-m_t[h]))
            ms = jnp.mean(hh * hh, axis=-1, keepdims=True)
            y = hh * lax.rsqrt(ms + EPS) * g_ref[:, h * dv:(h + 1) * dv]
            y = y * jax.nn.sigmoid(mo_ref[rows, h * dv:(h + 1) * dv])
            o_ref[rows, h * dv:(h + 1) * dv] = y.astype(o_ref.dtype)

        sv, m_new, carry_scale = [], [], []
        for h in hs:
            b_last = b_col[h][chunk - 1:chunk, :]
            a_col = b_last - b_col[h] + li_all[:, h:h + 1]
            m_new.append(jnp.maximum(b_last + m_prev[h], jnp.max(a_col, axis=0, keepdims=True)))
            carry_scale.append(jnp.exp(b_last + m_prev[h] - m_new[h]))
            src = jnp.exp(a_col - m_new[h])
            sv.append((src * v_ext[h].astype(F32)).astype(BF16))
        upd = [lax.dot_general(k[h], sv[h], (((0,), (0,)), ((), ())), preferred_element_type=F32)
               for h in hs]
        for h in hs:
            cn_ref[h] = carry_scale[h] * cn[h] + upd[h]
            m_ref[h:h + 1, :] = jnp.broadcast_to(m_new[h], (1, LANES))
        return carry

    lax.fori_loop(0, n_chunks, chunk_body, 0)


def mlstm(q, k, v, v_col0, mo, mo_col0, b, li, r_rows, g_out, heads, dk, dv, chunk):
    s = q.shape[0]
    rows = _pick(s, (512, 256, 128, 64))
    n_chunks = rows // chunk
    wv = heads * dv
    assert v_col0 % wv == 0 and mo_col0 % wv == 0
    v_blk, mo_blk = v_col0 // wv, mo_col0 // wv
    return pl.pallas_call(
        functools.partial(_mlstm_kernel, heads=heads, dk=dk, dv=dv, chunk=chunk, n_chunks=n_chunks),
        grid=(s // rows,),
        in_specs=[pl.BlockSpec((rows, heads * dk), lambda i: (i, 0)),
                  pl.BlockSpec((rows, heads * dk), lambda i: (i, 0)),
                  pl.BlockSpec((rows, wv), lambda i: (i, v_blk)),
                  pl.BlockSpec((rows, wv), lambda i: (i, mo_blk)),
                  pl.BlockSpec((rows, LANES), lambda i: (i, 0)),
                  pl.BlockSpec((rows, LANES), lambda i: (i, 0)),
                  pl.BlockSpec((n_chunks, heads, chunk), lambda i: (i, 0, 0)),
                  pl.BlockSpec((1, wv), lambda i: (0, 0))],
        out_specs=pl.BlockSpec((rows, wv), lambda i: (i, 0)),
        out_shape=jax.ShapeDtypeStruct((s, wv), BF16),
        scratch_shapes=[pltpu.VMEM((heads, dk, dv + LANES), F32),
                        pltpu.VMEM((heads, LANES), F32)],
        compiler_params=_params(1),
        name="mlstm",
    )(q, k, v, mo, b, li, r_rows, g_out.reshape(1, wv).astype(F32))


def _sb_kernel(q_ref, k_ref, v_ref, o_ref, acc_ref, run_ref, live_ref, *, tq, tk, hp, d):
    i = pl.program_id(1)
    groups = tq // tk
    row = lax.broadcasted_iota(jnp.int32, (tk, tk), 0)
    col = lax.broadcasted_iota(jnp.int32, (tk, tk), 1)
    diag_strict = col < row
    sums_rhs = jnp.concatenate([(row > col).astype(BF16), jnp.ones((tk, tk), BF16)], axis=1)
    sums_rhs = jnp.concatenate([sums_rhs, sums_rhs], axis=0)

    def walk(tiles):
        zs = [_dot_nt(q, kk) for q, kk, _, _, _ in tiles]
        log_betas, splits = [], []
        for z, (_, _, _, _, mask) in zip(zs, tiles):
            log_beta = jnp.minimum(z, 0.0) - jnp.log(1.0 + jnp.exp(-jnp.abs(z)))
            log_rest = log_beta - z
            if mask is not None:
                log_rest = jnp.where(mask, log_rest, 0.0)
            hi = log_rest.astype(BF16)
            lo = (log_rest - hi.astype(F32)).astype(BF16)
            log_betas.append(log_beta)
            splits.append(jnp.concatenate([hi, lo], axis=1))
        sums = [_dot(x, sums_rhs) for x in splits]
        weights, runs = [], []
        for log_beta, sm, (_, _, _, run, mask) in zip(log_betas, sums, tiles):
            log_w = log_beta + sm[:, :tk]
            wts = jnp.exp(log_w if run is None else log_w + run)
            if mask is not None:
                wts = jnp.where(mask, wts, 0.0)
            weights.append(wts.astype(BF16))
            runs.append(sm[:, tk:] if run is None else run + sm[:, tk:])
        outs = [_dot(w, vv) for w, (_, _, vv, _, _) in zip(weights, tiles)]
        return list(zip(outs, runs))

    def key_rows(blk):
        return pl.ds(pl.multiple_of(blk * tk, tk), tk)

    def cols(h):
        return slice(h * d, (h + 1) * d)

    def rows(g):
        return slice(g * tk, (g + 1) * tk)

    acc = {}
    run = {}
    for r in range(groups):
        ids = [(h, g) for h in range(hp) for g in range(r, groups)]
        tiles = []
        for h, g in ids:
            kr = key_rows(i * groups + g - r)
            tiles.append((q_ref[rows(g), cols(h)], k_ref[kr, cols(h)], v_ref[kr, cols(h)],
                          run.get((h, g)), diag_strict if r == 0 else None))
        for key, (out, new_run) in zip(ids, walk(tiles)):
            acc[key] = out if r == 0 else acc[key] + out
            run[key] = new_run
    for (h, g), value in acc.items():
        acc_ref[h, rows(g), :] = value
        run_ref[h, rows(g), :] = run[(h, g)]
    live_ref[0] = (jnp.max(run_ref[...]) > SB_DONE_LOG).astype(jnp.int32)

    n_prev = i * groups

    def block_body(step, carry):
        @pl.when(live_ref[0] > 0)
        def _():
            kr = key_rows(n_prev - 1 - step)
            tiles = [(q_ref[:, cols(h)], k_ref[kr, cols(h)], v_ref[kr, cols(h)], run_ref[h], None)
                     for h in range(hp)]
            for h, (out, new_run) in enumerate(walk(tiles)):
                acc_ref[h] += out
                run_ref[h] = new_run
            live_ref[0] = (jnp.max(run_ref[...]) > SB_DONE_LOG).astype(jnp.int32)
        return carry

    lax.fori_loop(0, n_prev, block_body, 0)
    for h in range(hp):
        o_ref[:, cols(h)] = acc_ref[h].astype(o_ref.dtype)


def stick_breaking(qkv, col0, heads, d):
    s = qkv.shape[0]
    tq = _pick(s, (256, 128))
    tk = 128
    hp = SB_HEADS_PER_STEP
    assert heads % hp == 0 and col0 % (hp * d) == 0
    n_hg = heads // hp
    blk0 = col0 // (hp * d)
    return pl.pallas_call(
        functools.partial(_sb_kernel, tq=tq, tk=tk, hp=hp, d=d),
        grid=(n_hg, s // tq),
        in_specs=[pl.BlockSpec((tq, hp * d), lambda hg, i: (i, blk0 + hg)),
                  pl.BlockSpec((s, hp * d), lambda hg, i: (0, blk0 + n_hg + hg)),
                  pl.BlockSpec((s, hp * d), lambda hg, i: (0, blk0 + 2 * n_hg + hg))],
        out_specs=pl.BlockSpec((tq, hp * d), lambda hg, i: (i, hg)),
        out_shape=jax.ShapeDtypeStruct((s, heads * d), BF16),
        scratch_shapes=[pltpu.VMEM((hp, tq, d), F32),
                        pltpu.VMEM((hp, tq, tk), F32),
                        pltpu.SMEM((1,), jnp.int32)],
        compiler_params=_params(2),
        name="stick_breaking",
    )(qkv, qkv, qkv)


def _ffn(x, g, w1, w3, w2):
    h = rmsnorm(x, g, BF16)
    u, w2b = gateup(h, w1, w3, w2)
    return matmul_residual(u, w2b, x, 0.5, tm_max=512, tn_max=512)


def _mixers(x, g_mix, w_in, conv_qk, b_igate, b_fgate, g_mlstm_out, w_proj_a, w_proj_b, w_out):
    s, d_model = x.shape
    qk_w = ML_HEADS * ML_QK_DIM
    ml_w = ML_HEADS * ML_V_DIM
    sb_w = SB_HEADS * SB_HEAD_DIM
    o_mv = 2 * qk_w
    o_mo = o_mv + ml_w
    o_mi = o_mo + ml_w
    o_mf = o_mi + ML_HEADS
    o_sq = o_mf + ML_HEADS
    o_ga = o_sq + 3 * sb_w
    wt = jnp.swapaxes(w_in, 0, 1)
    zrows = jnp.zeros((LANES - ML_HEADS, d_model), BF16)
    wt_gate = jnp.concatenate([wt[o_mi:o_mf].astype(BF16), zrows,
                               wt[o_mf:o_sq].astype(BF16), zrows], axis=0)
    scale16 = jnp.concatenate([jnp.ones((ml_w,), F32),
                               jnp.full((sb_w,), SB_HEAD_DIM ** -0.5, F32),
                               jnp.ones((2 * sb_w,), F32)])

    h = rmsnorm(x, g_mix, BF16)
    p32 = project(h, wt, [(0, o_mv), (o_mo, ml_w), (o_ga, 2 * d_model)], F32)
    p16 = project(h, wt, [(o_mv, ml_w), (o_sq, 3 * sb_w)], BF16, col_scale=scale16)
    pg = project(h, wt_gate, [(0, 2 * LANES)], F32)

    q, k = conv_silu_qk(p32, 2 * qk_w, conv_qk, ML_QK_DIM ** -0.5)
    b, li, r = gate_prep(pg, b_igate, b_fgate, ML_CHUNK)
    r_rows = jnp.swapaxes(r[:, :ML_HEADS].reshape(s // ML_CHUNK, ML_CHUNK, ML_HEADS), 1, 2)
    ya = mlstm(q, k, p16, 0, p32, 2 * qk_w, b, li, r_rows, g_mlstm_out,
               ML_HEADS, ML_QK_DIM, ML_V_DIM, ML_CHUNK)

    yb = stick_breaking(p16, ml_w, SB_HEADS, SB_HEAD_DIM)

    merged = gated_merge(ya, yb, w_proj_a, w_proj_b, p32, 2 * qk_w + ml_w, 2 * qk_w + ml_w + d_model)
    return matmul_residual(merged, w_out, x, 1.0, tm_max=1024, tn_max=512)


def kernel(x, g_ffn1, w1_ffn1, w3_ffn1, w2_ffn1, g_mix, w_in, conv_qk, b_igate, b_fgate,
           g_mlstm_out, w_proj_a, w_proj_b, w_out, g_ffn2, w1_ffn2, w3_ffn2, w2_ffn2, g_final):
    batch, seq, d_model = x.shape
    outs = []
    for bi in range(batch):
        xb = x[bi]
        for l in range(g_ffn1.shape[0]):
            xb = _ffn(xb, g_ffn1[l], w1_ffn1[l], w3_ffn1[l], w2_ffn1[l])
            xb = _mixers(xb, g_mix[l], w_in[l], conv_qk[l], b_igate[l], b_fgate[l],
                         g_mlstm_out[l], w_proj_a[l], w_proj_b[l], w_out[l])
            xb = _ffn(xb, g_ffn2[l], w1_ffn2[l], w3_ffn2[l], w2_ffn2[l])
        outs.append(rmsnorm(xb, g_final, x.dtype))
    return jnp.stack(outs, axis=0)
```

```python
import functools

import jax
import jax.numpy as jnp
from jax import lax
from jax.experimental import pallas as pl
from jax.experimental.pallas import tpu as pltpu

F32 = jnp.float32
BF16 = jnp.bfloat16
EPS = 1e-6

V7X_VMEM_BYTES = 64 * 1024 * 1024
VMEM_LIMIT_BYTES = V7X_VMEM_BYTES - 8 * 1024 * 1024
LANES = 128
BF16_SUBLANES = 16

ML_HEADS = 8
ML_QK_DIM = 128
ML_V_DIM = 256
ML_CHUNK = 64
SB_HEADS = 16
SB_HEAD_DIM = 128
SB_HEADS_PER_STEP = 4
CONV_WIDTH = 4
SB_DONE_LOG = -100.0


def _params(n_grid):
    return pltpu.CompilerParams(
        dimension_semantics=("arbitrary",) * n_grid,
        vmem_limit_bytes=VMEM_LIMIT_BYTES)


def _pick(n, candidates):
    for c in candidates:
        if n % c == 0:
            return c
    raise ValueError(f"no tile in {candidates} divides {n}")


def _log_sigmoid(x):
    return jnp.minimum(x, 0.0) - jnp.log1p(jnp.exp(-jnp.abs(x)))


def _dot(a, b):
    return jnp.dot(a, b, preferred_element_type=F32)


def _dot_nt(a, b):
    return lax.dot_general(a, b, (((1,), (1,)), ((), ())), preferred_element_type=F32)


def _rmsnorm_kernel(x_ref, g_ref, o_ref):
    x = x_ref[...]
    ms = jnp.mean(x * x, axis=-1, keepdims=True)
    o_ref[...] = (x * lax.rsqrt(ms + EPS) * g_ref[...]).astype(o_ref.dtype)


def rmsnorm(x, g, out_dtype):
    s, d = x.shape
    tr = _pick(s, (256, 128, 64, 8))
    return pl.pallas_call(
        _rmsnorm_kernel,
        grid=(s // tr,),
        in_specs=[pl.BlockSpec((tr, d), lambda i: (i, 0)),
                  pl.BlockSpec((1, d), lambda i: (0, 0))],
        out_specs=pl.BlockSpec((tr, d), lambda i: (i, 0)),
        out_shape=jax.ShapeDtypeStruct((s, d), out_dtype),
        compiler_params=_params(1),
        name="rmsnorm",
    )(x, g.reshape(1, d).astype(F32))


def _gateup_kernel(a_ref, w1_ref, w3_ref, w2_ref, o_ref, w2b_ref):
    a = a_ref[...]
    g = _dot(a, w1_ref[...].astype(BF16))
    u = _dot(a, w3_ref[...].astype(BF16))
    o_ref[...] = (g * jax.nn.sigmoid(g) * u).astype(o_ref.dtype)
    w2b_ref[...] = w2_ref[...].astype(BF16)


def gateup(h, w1, w3, w2):
    m, k = h.shape
    n = w1.shape[1]
    k2, n2 = w2.shape
    tm = _pick(m, (2048, 1024, 512, 256, 128, 64, 8))
    tn = _pick(n, (256, 128))
    n_i, n_j = m // tm, n // tn
    slab = k2 // (n_i * n_j)
    assert slab * n_i * n_j == k2 and slab % BF16_SUBLANES == 0, (k2, n_i, n_j)
    return pl.pallas_call(
        _gateup_kernel,
        grid=(n_i, n_j),
        in_specs=[pl.BlockSpec((tm, k), lambda i, j: (i, 0), pipeline_mode=pl.Buffered(1)),
                  pl.BlockSpec((k, tn), lambda i, j: (0, j)),
                  pl.BlockSpec((k, tn), lambda i, j: (0, j)),
                  pl.BlockSpec((slab, n2), lambda i, j: (i * n_j + j, 0))],
        out_specs=[pl.BlockSpec((tm, tn), lambda i, j: (i, j)),
                   pl.BlockSpec((slab, n2), lambda i, j: (i * n_j + j, 0))],
        out_shape=[jax.ShapeDtypeStruct((m, n), BF16),
                   jax.ShapeDtypeStruct((k2, n2), BF16)],
        compiler_params=_params(2),
        name="ffn_gateup",
    )(h, w1, w3, w2)


def _resid_kernel(a_ref, w_ref, x_ref, o_ref, *, alpha):
    o_ref[...] = x_ref[...] + alpha * _dot(a_ref[...], w_ref[...].astype(BF16))


def matmul_residual(a, w, x, alpha, tm_max, tn_max):
    m, k = a.shape
    n = w.shape[1]
    tm = _pick(m, tuple(t for t in (1024, 512, 256, 128, 64, 8) if t <= tm_max))
    tn = _pick(n, tuple(t for t in (512, 256, 128) if t <= tn_max))
    return pl.pallas_call(
        functools.partial(_resid_kernel, alpha=alpha),
        grid=(m // tm, n // tn),
        in_specs=[pl.BlockSpec((tm, k), lambda i, j: (i, 0)),
                  pl.BlockSpec((k, tn), lambda i, j: (0, j)),
                  pl.BlockSpec((tm, tn), lambda i, j: (i, j))],
        out_specs=pl.BlockSpec((tm, tn), lambda i, j: (i, j)),
        out_shape=jax.ShapeDtypeStruct((m, n), F32),
        compiler_params=_params(2),
        name="matmul_residual",
    )(a, w, x)


def _proj_kernel(a_ref, wt_ref, o_ref):
    o_ref[...] = _dot_nt(a_ref[...], wt_ref[...].astype(BF16)).astype(o_ref.dtype)


def _proj_scaled_kernel(a_ref, wt_ref, s_ref, o_ref):
    o_ref[...] = (_dot_nt(a_ref[...], wt_ref[...].astype(BF16)) * s_ref[...]).astype(o_ref.dtype)


def project(a, wt, segments, out_dtype, col_scale=None):
    m, k = a.shape
    n = sum(seg_n for _, seg_n in segments)
    tm = _pick(m, (1024, 512, 256, 128, 64, 8))
    tn = 512 if all(seg_n % 512 == 0 for _, seg_n in segments) else 256
    assert all(seg_n % tn == 0 and row0 % 8 == 0 for row0, seg_n in segments)

    def wt_row(j):
        blk0 = 0
        row8 = None
        for row0, seg_n in segments:
            here = row0 // 8 + (j - blk0) * (tn // 8)
            row8 = here if row8 is None else jnp.where(j >= blk0, here, row8)
            blk0 += seg_n // tn
        return row8 * 8

    in_specs = [pl.BlockSpec((tm, k), lambda i, j: (i, 0)),
                pl.BlockSpec((pl.Element(tn), pl.Element(k)), lambda i, j: (wt_row(j), 0))]
    args = [a, wt]
    body = _proj_kernel
    if col_scale is not None:
        in_specs.append(pl.BlockSpec((1, tn), lambda i, j: (0, j)))
        args.append(col_scale.reshape(1, n).astype(F32))
        body = _proj_scaled_kernel
    return pl.pallas_call(
        body,
        grid=(m // tm, n // tn),
        in_specs=in_specs,
        out_specs=pl.BlockSpec((tm, tn), lambda i, j: (i, j)),
        out_shape=jax.ShapeDtypeStruct((m, n), out_dtype),
        compiler_params=_params(2),
        name="in_project",
    )(*args)


def _merge_kernel(ya_ref, yb_ref, wa_ref, wb_ref, ga_ref, gb_ref, o_ref):
    pa = _dot(ya_ref[...], wa_ref[...].astype(BF16))
    pb = _dot(yb_ref[...], wb_ref[...].astype(BF16))
    o_ref[...] = (jax.nn.sigmoid(ga_ref[...]) * pa
                  + jax.nn.sigmoid(gb_ref[...]) * pb).astype(o_ref.dtype)


def gated_merge(ya, yb, wa, wb, gates, ga_col0, gb_col0):
    m, k = ya.shape
    n = wa.shape[1]
    tm = _pick(m, (1024, 512, 256, 128, 64, 8))
    tn = _pick(n, (512, 256, 128))
    assert ga_col0 % tn == 0 and gb_col0 % tn == 0
    ga_blk, gb_blk = ga_col0 // tn, gb_col0 // tn
    return pl.pallas_call(
        _merge_kernel,
        grid=(m // tm, n // tn),
        in_specs=[pl.BlockSpec((tm, k), lambda i, j: (i, 0)),
                  pl.BlockSpec((tm, k), lambda i, j: (i, 0)),
                  pl.BlockSpec((k, tn), lambda i, j: (0, j)),
                  pl.BlockSpec((k, tn), lambda i, j: (0, j)),
                  pl.BlockSpec((tm, tn), lambda i, j: (i, ga_blk + j)),
                  pl.BlockSpec((tm, tn), lambda i, j: (i, gb_blk + j))],
        out_specs=pl.BlockSpec((tm, tn), lambda i, j: (i, j)),
        out_shape=jax.ShapeDtypeStruct((m, n), BF16),
        compiler_params=_params(2),
        name="gated_merge",
    )(ya, yb, wa, wb, gates, gates)


def _conv_kernel(cur_ref, prev_ref, w_ref, q_ref, k_ref, *, k_scale):
    cur = cur_ref[...]
    prev = jnp.where(pl.program_id(0) > 0, prev_ref[...], 0.0)
    ext = jnp.concatenate([prev, cur], axis=0)
    w = w_ref[...]
    tr = cur.shape[0]
    acc = w[CONV_WIDTH - 1:CONV_WIDTH] * cur
    for j in range(CONV_WIDTH - 1):
        off = 8 - (CONV_WIDTH - 1) + j
        acc = acc + w[j:j + 1] * ext[off:off + tr]
    y = acc * jax.nn.sigmoid(acc)
    half = y.shape[1] // 2
    q_ref[...] = y[:, :half].astype(q_ref.dtype)
    k_ref[...] = (y[:, half:] * k_scale).astype(k_ref.dtype)


def conv_silu_qk(p, width2, conv_w, k_scale):
    s = p.shape[0]
    tr = _pick(s, (256, 128, 64, 8))
    half = width2 // 2
    return pl.pallas_call(
        functools.partial(_conv_kernel, k_scale=k_scale),
        grid=(s // tr,),
        in_specs=[pl.BlockSpec((tr, width2), lambda i: (i, 0)),
                  pl.BlockSpec((8, width2), lambda i: (jnp.maximum(i * (tr // 8) - 1, 0), 0)),
                  pl.BlockSpec((CONV_WIDTH, width2), lambda i: (0, 0))],
        out_specs=[pl.BlockSpec((tr, half), lambda i: (i, 0)),
                   pl.BlockSpec((tr, half), lambda i: (i, 0))],
        out_shape=[jax.ShapeDtypeStruct((s, half), BF16),
                   jax.ShapeDtypeStruct((s, half), BF16)],
        compiler_params=_params(1),
        name="conv_silu_qk",
    )(p, p, conv_w.astype(F32))


def _split3(x):
    hi = x.astype(BF16)
    rem = x - hi.astype(F32)
    mid = rem.astype(BF16)
    lo = (rem - mid.astype(F32)).astype(BF16)
    return jnp.concatenate([hi, mid, lo], axis=1)


def _gate_kernel(g_ref, bi_ref, bf_ref, e_ref, r_ref, brep_ref, rmaxrep_ref, arep_ref, *, chunk):
    g = g_ref[...]
    rows = g.shape[0]
    li = g[:, :LANES] + bi_ref[...]
    lf = _log_sigmoid(g[:, LANES:] + bf_ref[...])
    pos = lax.broadcasted_iota(jnp.int32, lf.shape, 0) % chunk

    def scan(x, op, fill, down):
        d = 1
        while d < chunk:
            if down:
                x = op(x, jnp.where(pos >= d, pltpu.roll(x, d, 0), fill))
            else:
                x = op(x, jnp.where(pos + d < chunk, pltpu.roll(x, rows - d, 0), fill))
            d *= 2
        return x

    b = scan(lf, jnp.add, 0.0, True)
    r = li - b
    rmax = scan(r, jnp.maximum, -jnp.inf, True)
    a = (scan(lf, jnp.add, 0.0, False) - lf) + li
    r_ref[...] = r
    for x, ref in ((b, brep_ref), (rmax, rmaxrep_ref), (a, arep_ref)):
        ref[...] = _dot(_split3(x), e_ref[...])


def gate_prep(g, b_i, b_f, chunk, heads):
    s = g.shape[0]
    tr = _pick(s, (512, 256, 128, 64))
    pad = lambda v: jnp.zeros((1, LANES), F32).at[0, :v.shape[0]].set(v.astype(F32))
    src_lane = lax.broadcasted_iota(jnp.int32, (3 * LANES, heads * LANES), 0) % LANES
    dst_head = lax.broadcasted_iota(jnp.int32, (3 * LANES, heads * LANES), 1) // LANES
    expand = (src_lane == dst_head).astype(BF16)
    rep = jax.ShapeDtypeStruct((s, heads * LANES), F32)
    return pl.pallas_call(
        functools.partial(_gate_kernel, chunk=chunk),
        grid=(s // tr,),
        in_specs=[pl.BlockSpec((tr, 2 * LANES), lambda i: (i, 0)),
                  pl.BlockSpec((1, LANES), lambda i: (0, 0)),
                  pl.BlockSpec((1, LANES), lambda i: (0, 0)),
                  pl.BlockSpec((3 * LANES, heads * LANES), lambda i: (0, 0))],
        out_specs=[pl.BlockSpec((tr, LANES), lambda i: (i, 0))]
                  + [pl.BlockSpec((tr, heads * LANES), lambda i: (i, 0))] * 3,
        out_shape=[jax.ShapeDtypeStruct((s, LANES), F32), rep, rep, rep],
        compiler_params=_params(1),
        name="gate_prep",
    )(g, pad(b_i), pad(b_f), expand)


def _mlstm_kernel(q_ref, k_ref, v_ref, mo_ref, brep_ref, rmaxrep_ref, arep_ref, r_ref, g_ref, o_ref,
                  cn_ref, m_ref, *, heads, dk, dv, chunk, n_chunks):
    @pl.when(pl.program_id(0) == 0)
    def _():
        cn_ref[...] = jnp.zeros_like(cn_ref)
        m_ref[...] = jnp.zeros_like(m_ref)

    row = lax.broadcasted_iota(jnp.int32, (chunk, chunk), 0)
    col = lax.broadcasted_iota(jnp.int32, (chunk, chunk), 1)
    tril = row >= col
    ones_ext = jnp.ones((chunk, LANES), BF16)
    ones_sum = jnp.ones((2 * dv, LANES), BF16)
    v_tiles = dv // LANES

    def lanes(h):
        return slice(h * LANES, (h + 1) * LANES)

    def rep(x, n):
        return jnp.concatenate([x] * n, axis=1)

    def chunk_body(c, carry):
        r0 = pl.multiple_of(c * chunk, chunk)
        rows = pl.ds(r0, chunk)
        tail = pl.ds(pl.multiple_of(r0 + chunk - 8, 8), 8)
        r_all = r_ref[c]
        hs = range(heads)
        q = [q_ref[rows, h * dk:(h + 1) * dk] for h in hs]
        k = [k_ref[rows, h * dk:(h + 1) * dk] for h in hs]
        v_ext = [jnp.concatenate([v_ref[rows, h * dv:(h + 1) * dv], ones_ext], axis=1) for h in hs]
        cn = [cn_ref[h] for h in hs]
        s = [_dot_nt(q[h], k[h]) for h in hs]
        inter = [_dot(q[h], cn[h].astype(BF16)) for h in hs]

        m_prev = [m_ref[h:h + 1, :] for h in hs]
        big_m = [jnp.maximum(m_prev[h], rmaxrep_ref[rows, lanes(h)]) for h in hs]
        w = []
        for h in hs:
            d_exp = jnp.exp(r_all[h:h + 1, :] - big_m[h][:, :chunk])
            w.append((s[h] * jnp.where(tril, d_exp, 0.0)).astype(BF16))
        intra = [_dot(w[h], v_ext[h]) for h in hs]

        for h in hs:
            decay = jnp.exp(m_prev[h] - big_m[h])
            tot = rep(decay, v_tiles + 1) * inter[h] + intra[h]
            floor = jnp.exp(-(brep_ref[rows, lanes(h)] + big_m[h]))
            denom = jnp.maximum(jnp.abs(tot[:, dv:]), floor)
            hh = tot[:, :dv] / rep(denom, v_tiles)
            sq = hh * hh
            sq_hi = sq.astype(BF16)
            sq_lo = (sq - sq_hi.astype(F32)).astype(BF16)
            ms = _dot(jnp.concatenate([sq_hi, sq_lo], axis=1), ones_sum) * (1.0 / dv)
            y = hh * rep(lax.rsqrt(ms + EPS), v_tiles) * g_ref[:, h * dv:(h + 1) * dv]
            y = y * jax.nn.sigmoid(mo_ref[rows, h * dv:(h + 1) * dv])
            o_ref[rows, h * dv:(h + 1) * dv] = y.astype(o_ref.dtype)

        sv, m_new, carry_scale = [], [], []
        for h in hs:
            a_rep = arep_ref[rows, lanes(h)]
            b_last = brep_ref[tail, lanes(h)][7:8, :]
            m_new.append(jnp.maximum(b_last + m_prev[h], jnp.max(a_rep, axis=0, keepdims=True)))
            carry_scale.append(jnp.exp(b_last + m_prev[h] - m_new[h]))
            src = jnp.exp(a_rep - m_new[h])
            sv.append((rep(src, v_tiles + 1) * v_ext[h].astype(F32)).astype(BF16))
        upd = [lax.dot_general(k[h], sv[h], (((0,), (0,)), ((), ())), preferred_element_type=F32)
               for h in hs]
        for h in hs:
            cn_ref[h] = rep(carry_scale[h], v_tiles + 1) * cn[h] + upd[h]
            m_ref[h:h + 1, :] = m_new[h]
        return carry

    lax.fori_loop(0, n_chunks, chunk_body, 0)


def mlstm(q, k, v, v_col0, mo, mo_col0, b_rep, rmax_rep, a_rep, r_rows, g_out, heads, dk, dv, chunk):
    s = q.shape[0]
    rows = _pick(s, (512, 256, 128, 64))
    n_chunks = rows // chunk
    wv = heads * dv
    assert v_col0 % wv == 0 and mo_col0 % wv == 0 and dv % LANES == 0
    v_blk, mo_blk = v_col0 // wv, mo_col0 // wv
    rep_spec = pl.BlockSpec((rows, heads * LANES), lambda i: (i, 0))
    return pl.pallas_call(
        functools.partial(_mlstm_kernel, heads=heads, dk=dk, dv=dv, chunk=chunk, n_chunks=n_chunks),
        grid=(s // rows,),
        in_specs=[pl.BlockSpec((rows, heads * dk), lambda i: (i, 0)),
                  pl.BlockSpec((rows, heads * dk), lambda i: (i, 0)),
                  pl.BlockSpec((rows, wv), lambda i: (i, v_blk)),
                  pl.BlockSpec((rows, wv), lambda i: (i, mo_blk)),
                  rep_spec, rep_spec, rep_spec,
                  pl.BlockSpec((n_chunks, heads, chunk), lambda i: (i, 0, 0)),
                  pl.BlockSpec((1, wv), lambda i: (0, 0))],
        out_specs=pl.BlockSpec((rows, wv), lambda i: (i, 0)),
        out_shape=jax.ShapeDtypeStruct((s, wv), BF16),
        scratch_shapes=[pltpu.VMEM((heads, dk, dv + LANES), F32),
                        pltpu.VMEM((heads, LANES), F32)],
        compiler_params=_params(1),
        name="mlstm",
    )(q, k, v, mo, b_rep, rmax_rep, a_rep, r_rows, g_out.reshape(1, wv).astype(F32))


def _sb_kernel(q_ref, k_ref, v_ref, o_ref, acc_ref, run_ref, live_ref, *, tq, tk, hp, d):
    i = pl.program_id(1)
    groups = tq // tk
    row = lax.broadcasted_iota(jnp.int32, (tk, tk), 0)
    col = lax.broadcasted_iota(jnp.int32, (tk, tk), 1)
    diag_strict = col < row
    sums_rhs = jnp.concatenate([(row > col).astype(BF16), jnp.ones((tk, tk), BF16)], axis=1)
    sums_rhs = jnp.concatenate([sums_rhs, sums_rhs], axis=0)

    def walk(tiles):
        zs = [_dot_nt(q, kk) for q, kk, _, _, _ in tiles]
        log_betas, splits = [], []
        for z, (_, _, _, _, mask) in zip(zs, tiles):
            log_beta = jnp.minimum(z, 0.0) - jnp.log(1.0 + jnp.exp(-jnp.abs(z)))
            log_rest = log_beta - z
            if mask is not None:
                log_rest = jnp.where(mask, log_rest, 0.0)
            hi = log_rest.astype(BF16)
            lo = (log_rest - hi.astype(F32)).astype(BF16)
            log_betas.append(log_beta)
            splits.append(jnp.concatenate([hi, lo], axis=1))
        sums = [_dot(x, sums_rhs) for x in splits]
        weights, runs = [], []
        for log_beta, sm, (_, _, _, run, mask) in zip(log_betas, sums, tiles):
            log_w = log_beta + sm[:, :tk]
            wts = jnp.exp(log_w if run is None else log_w + run)
            if mask is not None:
                wts = jnp.where(mask, wts, 0.0)
            weights.append(wts.astype(BF16))
            runs.append(sm[:, tk:] if run is None else run + sm[:, tk:])
        outs = [_dot(w, vv) for w, (_, _, vv, _, _) in zip(weights, tiles)]
        return list(zip(outs, runs))

    def key_rows(blk):
        return pl.ds(pl.multiple_of(blk * tk, tk), tk)

    def cols(h):
        return slice(h * d, (h + 1) * d)

    def rows(g):
        return slice(g * tk, (g + 1) * tk)

    acc = {}
    run = {}
    for r in range(groups):
        ids = [(h, g) for h in range(hp) for g in range(r, groups)]
        tiles = []
        for h, g in ids:
            kr = key_rows(i * groups + g - r)
            tiles.append((q_ref[rows(g), cols(h)], k_ref[kr, cols(h)], v_ref[kr, cols(h)],
                          run.get((h, g)), diag_strict if r == 0 else None))
        for key, (out, new_run) in zip(ids, walk(tiles)):
            acc[key] = out if r == 0 else acc[key] + out
            run[key] = new_run
    for (h, g), value in acc.items():
        acc_ref[h, rows(g), :] = value
        run_ref[h, rows(g), :] = run[(h, g)]
    live_ref[0] = (jnp.max(run_ref[...]) > SB_DONE_LOG).astype(jnp.int32)

    n_prev = i * groups

    def block_body(step, carry):
        @pl.when(live_ref[0] > 0)
        def _():
            kr = key_rows(n_prev - 1 - step)
            tiles = [(q_ref[:, cols(h)], k_ref[kr, cols(h)], v_ref[kr, cols(h)], run_ref[h], None)
                     for h in range(hp)]
            for h, (out, new_run) in enumerate(walk(tiles)):
                acc_ref[h] += out
                run_ref[h] = new_run
            live_ref[0] = (jnp.max(run_ref[...]) > SB_DONE_LOG).astype(jnp.int32)
        return carry

    lax.fori_loop(0, n_prev, block_body, 0)
    for h in range(hp):
        o_ref[:, cols(h)] = acc_ref[h].astype(o_ref.dtype)


def stick_breaking(qkv, col0, heads, d):
    s = qkv.shape[0]
    tq = _pick(s, (256, 128))
    tk = 128
    hp = SB_HEADS_PER_STEP
    assert heads % hp == 0 and col0 % (hp * d) == 0
    n_hg = heads // hp
    blk0 = col0 // (hp * d)
    return pl.pallas_call(
        functools.partial(_sb_kernel, tq=tq, tk=tk, hp=hp, d=d),
        grid=(n_hg, s // tq),
        in_specs=[pl.BlockSpec((tq, hp * d), lambda hg, i: (i, blk0 + hg)),
                  pl.BlockSpec((s, hp * d), lambda hg, i: (0, blk0 + n_hg + hg)),
                  pl.BlockSpec((s, hp * d), lambda hg, i: (0, blk0 + 2 * n_hg + hg))],
        out_specs=pl.BlockSpec((tq, hp * d), lambda hg, i: (i, hg)),
        out_shape=jax.ShapeDtypeStruct((s, heads * d), BF16),
        scratch_shapes=[pltpu.VMEM((hp, tq, d), F32),
                        pltpu.VMEM((hp, tq, tk), F32),
                        pltpu.SMEM((1,), jnp.int32)],
        compiler_params=_params(2),
        name="stick_breaking",
    )(qkv, qkv, qkv)


def _ffn(x, g, w1, w3, w2):
    h = rmsnorm(x, g, BF16)
    u, w2b = gateup(h, w1, w3, w2)
    return matmul_residual(u, w2b, x, 0.5, tm_max=512, tn_max=512)


def _mixers(x, g_mix, w_in, conv_qk, b_igate, b_fgate, g_mlstm_out, w_proj_a, w_proj_b, w_out):
    s, d_model = x.shape
    qk_w = ML_HEADS * ML_QK_DIM
    ml_w = ML_HEADS * ML_V_DIM
    sb_w = SB_HEADS * SB_HEAD_DIM
    o_mv = 2 * qk_w
    o_mo = o_mv + ml_w
    o_mi = o_mo + ml_w
    o_mf = o_mi + ML_HEADS
    o_sq = o_mf + ML_HEADS
    o_ga = o_sq + 3 * sb_w
    wt = jnp.swapaxes(w_in, 0, 1)
    zrows = jnp.zeros((LANES - ML_HEADS, d_model), F32)
    wt_gate = jnp.concatenate([wt[o_mi:o_mf], zrows, wt[o_mf:o_sq], zrows], axis=0)
    scale16 = jnp.concatenate([jnp.ones((ml_w,), F32),
                               jnp.full((sb_w,), SB_HEAD_DIM ** -0.5, F32),
                               jnp.ones((2 * sb_w,), F32)])

    h = rmsnorm(x, g_mix, BF16)
    p32 = project(h, wt, [(0, o_mv), (o_mo, ml_w), (o_ga, 2 * d_model)], F32)
    p16 = project(h, wt, [(o_mv, ml_w), (o_sq, 3 * sb_w)], BF16, col_scale=scale16)
    pg = project(h, wt_gate, [(0, 2 * LANES)], F32)

    q, k = conv_silu_qk(p32, 2 * qk_w, conv_qk, ML_QK_DIM ** -0.5)
    r, b_rep, rmax_rep, a_rep = gate_prep(pg, b_igate, b_fgate, ML_CHUNK, ML_HEADS)
    r_rows = jnp.swapaxes(r[:, :ML_HEADS].reshape(s // ML_CHUNK, ML_CHUNK, ML_HEADS), 1, 2)
    ya = mlstm(q, k, p16, 0, p32, 2 * qk_w, b_rep, rmax_rep, a_rep, r_rows, g_mlstm_out,
               ML_HEADS, ML_QK_DIM, ML_V_DIM, ML_CHUNK)

    yb = stick_breaking(p16, ml_w, SB_HEADS, SB_HEAD_DIM)

    merged = gated_merge(ya, yb, w_proj_a, w_proj_b, p32, 2 * qk_w + ml_w, 2 * qk_w + ml_w + d_model)
    return matmul_residual(merged, w_out, x, 1.0, tm_max=1024, tn_max=512)


def kernel(x, g_ffn1, w1_ffn1, w3_ffn1, w2_ffn1, g_mix, w_in, conv_qk, b_igate, b_fgate,
           g_mlstm_out, w_proj_a, w_proj_b, w_out, g_ffn2, w1_ffn2, w3_ffn2, w2_ffn2, g_final):
    batch, seq, d_model = x.shape
    outs = []
    for bi in range(batch):
        xb = x[bi]
        for l in range(g_ffn1.shape[0]):
            xb = _ffn(xb, g_ffn1[l], w1_ffn1[l], w3_ffn1[l], w2_ffn1[l])
            xb = _mixers(xb, g_mix[l], w_in[l], conv_qk[l], b_igate[l], b_fgate[l],
                         g_mlstm_out[l], w_proj_a[l], w_proj_b[l], w_out[l])
            xb = _ffn(xb, g_ffn2[l], w1_ffn2[l], w3_ffn2[l], w2_ffn2[l])
        outs.append(rmsnorm(xb, g_final, x.dtype))
    return jnp.stack(outs, axis=0)
```

```python
import functools

import jax
import jax.numpy as jnp
from jax import lax
from jax.experimental import pallas as pl
from jax.experimental.pallas import tpu as pltpu

F32 = jnp.float32
BF16 = jnp.bfloat16
EPS = 1e-6

V7X_VMEM_BYTES = 64 * 1024 * 1024
VMEM_LIMIT_BYTES = V7X_VMEM_BYTES - 8 * 1024 * 1024
LANES = 128
BF16_SUBLANES = 16

ML_HEADS = 8
ML_QK_DIM = 128
ML_V_DIM = 256
ML_CHUNK = 64
SB_HEADS = 16
SB_HEAD_DIM = 128
SB_HEADS_PER_STEP = 8
SB_Q_ROWS = 128
CONV_WIDTH = 4
SB_DONE_LOG = -88.0


def _params(n_grid):
    return pltpu.CompilerParams(
        dimension_semantics=("arbitrary",) * n_grid,
        vmem_limit_bytes=VMEM_LIMIT_BYTES)


def _pick(n, candidates):
    for c in candidates:
        if n % c == 0:
            return c
    raise ValueError(f"no tile in {candidates} divides {n}")


def _log_sigmoid(x):
    return jnp.minimum(x, 0.0) - jnp.log1p(jnp.exp(-jnp.abs(x)))


def _dot(a, b):
    return jnp.dot(a, b, preferred_element_type=F32)


def _dot_nt(a, b):
    return lax.dot_general(a, b, (((1,), (1,)), ((), ())), preferred_element_type=F32)


def _rep(x, n):
    return x if n == 1 else jnp.concatenate([x] * n, axis=1)


def _rmsnorm_kernel(x_ref, g_ref, o_ref):
    x = x_ref[...]
    ms = jnp.mean(x * x, axis=-1, keepdims=True)
    o_ref[...] = (x * lax.rsqrt(ms + EPS) * g_ref[...]).astype(o_ref.dtype)


def rmsnorm(x, g, out_dtype):
    s, d = x.shape
    tr = _pick(s, (256, 128, 64, 8))
    return pl.pallas_call(
        _rmsnorm_kernel,
        grid=(s // tr,),
        in_specs=[pl.BlockSpec((tr, d), lambda i: (i, 0)),
                  pl.BlockSpec((1, d), lambda i: (0, 0))],
        out_specs=pl.BlockSpec((tr, d), lambda i: (i, 0)),
        out_shape=jax.ShapeDtypeStruct((s, d), out_dtype),
        compiler_params=_params(1),
        name="rmsnorm",
    )(x, g.reshape(1, d).astype(F32))


def _cast_stats_kernel(x_ref, g_ref, xb_ref, rstd_ref):
    x = x_ref[...]
    xb_ref[...] = (x * g_ref[...]).astype(BF16)
    ms = jnp.mean(x * x, axis=-1, keepdims=True)
    rstd_ref[...] = jnp.broadcast_to(lax.rsqrt(ms + EPS), rstd_ref.shape)


def cast_stats(x, gain):
    s, d = x.shape
    tr = _pick(s, (256, 128, 64, 8))
    return pl.pallas_call(
        _cast_stats_kernel,
        grid=(s // tr,),
        in_specs=[pl.BlockSpec((tr, d), lambda i: (i, 0)),
                  pl.BlockSpec((1, d), lambda i: (0, 0))],
        out_specs=[pl.BlockSpec((tr, d), lambda i: (i, 0)),
                   pl.BlockSpec((tr, LANES), lambda i: (i, 0))],
        out_shape=[jax.ShapeDtypeStruct((s, d), BF16),
                   jax.ShapeDtypeStruct((s, LANES), F32)],
        compiler_params=_params(1),
        name="cast_stats",
    )(x, gain.reshape(1, d).astype(F32))


def _gateup_kernel(a_ref, rstd_ref, w1_ref, w3_ref, w2_ref, o_ref, w2b_ref):
    a = a_ref[...]
    rstd = _rep(rstd_ref[...], o_ref.shape[1] // LANES)
    g = rstd * _dot(a, w1_ref[...].astype(BF16))
    u = rstd * _dot(a, w3_ref[...].astype(BF16))
    o_ref[...] = (g * jax.nn.sigmoid(g) * u).astype(o_ref.dtype)
    w2b_ref[...] = w2_ref[...].astype(BF16)


def gateup(xb, rstd, w1, w3, w2):
    m, k = xb.shape
    n = w1.shape[1]
    k2, n2 = w2.shape
    tm = _pick(m, (2048, 1024, 512, 256, 128, 64, 8))
    tn = _pick(n, (256, 128))
    n_i, n_j = m // tm, n // tn
    slab = k2 // (n_i * n_j)
    assert slab * n_i * n_j == k2 and slab % BF16_SUBLANES == 0, (k2, n_i, n_j)
    return pl.pallas_call(
        _gateup_kernel,
        grid=(n_i, n_j),
        in_specs=[pl.BlockSpec((tm, k), lambda i, j: (i, 0), pipeline_mode=pl.Buffered(1)),
                  pl.BlockSpec((tm, LANES), lambda i, j: (i, 0)),
                  pl.BlockSpec((k, tn), lambda i, j: (0, j)),
                  pl.BlockSpec((k, tn), lambda i, j: (0, j)),
                  pl.BlockSpec((slab, n2), lambda i, j: (i * n_j + j, 0))],
        out_specs=[pl.BlockSpec((tm, tn), lambda i, j: (i, j)),
                   pl.BlockSpec((slab, n2), lambda i, j: (i * n_j + j, 0))],
        out_shape=[jax.ShapeDtypeStruct((m, n), BF16),
                   jax.ShapeDtypeStruct((k2, n2), BF16)],
        compiler_params=_params(2),
        name="ffn_gateup",
    )(xb, rstd, w1, w3, w2)


def _resid_kernel(a_ref, w_ref, x_ref, o_ref, *, alpha):
    o_ref[...] = x_ref[...] + alpha * _dot(a_ref[...], w_ref[...].astype(BF16))


def _resid_stats_kernel(a_ref, w_ref, x_ref, g_ref, o_ref, ob_ref, rstd_ref, ssq_ref, *, alpha, n_total):
    j = pl.program_id(1)
    y = x_ref[...] + alpha * _dot(a_ref[...], w_ref[...].astype(BF16))
    o_ref[...] = y
    ob_ref[...] = (y * g_ref[...]).astype(BF16)
    sq = y * y
    part = sq[:, :LANES]
    for t in range(1, sq.shape[1] // LANES):
        part = part + sq[:, t * LANES:(t + 1) * LANES]

    @pl.when(j == 0)
    def _():
        ssq_ref[...] = part

    @pl.when(j > 0)
    def _():
        ssq_ref[...] += part

    @pl.when(j == pl.num_programs(1) - 1)
    def _():
        ms = jnp.sum(ssq_ref[...], axis=-1, keepdims=True) * (1.0 / n_total)
        rstd_ref[...] = jnp.broadcast_to(lax.rsqrt(ms + EPS), rstd_ref.shape)


def matmul_residual(a, w, x, alpha, tm_max, tn_max, next_gain=None):
    m, k = a.shape
    n = w.shape[1]
    tm = _pick(m, tuple(t for t in (1024, 512, 256, 128, 64, 8) if t <= tm_max))
    tn = _pick(n, tuple(t for t in (512, 256, 128) if t <= tn_max))
    tile = pl.BlockSpec((tm, tn), lambda i, j: (i, j))
    in_specs = [pl.BlockSpec((tm, k), lambda i, j: (i, 0)),
                pl.BlockSpec((k, tn), lambda i, j: (0, j)),
                tile]
    if next_gain is None:
        return pl.pallas_call(
            functools.partial(_resid_kernel, alpha=alpha),
            grid=(m // tm, n // tn),
            in_specs=in_specs,
            out_specs=tile,
            out_shape=jax.ShapeDtypeStruct((m, n), F32),
            compiler_params=_params(2),
            name="matmul_residual",
        )(a, w, x)
    return pl.pallas_call(
        functools.partial(_resid_stats_kernel, alpha=alpha, n_total=n),
        grid=(m // tm, n // tn),
        in_specs=in_specs + [pl.BlockSpec((1, tn), lambda i, j: (0, j))],
        out_specs=[tile, tile, pl.BlockSpec((tm, LANES), lambda i, j: (i, 0))],
        out_shape=[jax.ShapeDtypeStruct((m, n), F32),
                   jax.ShapeDtypeStruct((m, n), BF16),
                   jax.ShapeDtypeStruct((m, LANES), F32)],
        scratch_shapes=[pltpu.VMEM((tm, LANES), F32)],
        compiler_params=_params(2),
        name="matmul_residual_stats",
    )(a, w, x, next_gain.reshape(1, n).astype(F32))


def _proj_kernel(a_ref, rstd_ref, wt_ref, o_ref):
    y = _dot_nt(a_ref[...], wt_ref[...].astype(BF16))
    o_ref[...] = (_rep(rstd_ref[...], y.shape[1] // LANES) * y).astype(o_ref.dtype)


def _proj_scaled_kernel(a_ref, rstd_ref, wt_ref, s_ref, o_ref):
    y = _dot_nt(a_ref[...], wt_ref[...].astype(BF16))
    o_ref[...] = (_rep(rstd_ref[...], y.shape[1] // LANES) * y * s_ref[...]).astype(o_ref.dtype)


def project(xb, rstd, wt, segments, out_dtype, col_scale=None):
    m, k = xb.shape
    n = sum(seg_n for _, seg_n in segments)
    tm = _pick(m, (2048, 1024, 512, 256, 128, 64, 8))
    tn = 512 if all(seg_n % 512 == 0 for _, seg_n in segments) else 256
    assert all(seg_n % tn == 0 and row0 % 8 == 0 for row0, seg_n in segments)

    def wt_row(j):
        blk0 = 0
        row8 = None
        for row0, seg_n in segments:
            here = row0 // 8 + (j - blk0) * (tn // 8)
            row8 = here if row8 is None else jnp.where(j >= blk0, here, row8)
            blk0 += seg_n // tn
        return row8 * 8

    in_specs = [pl.BlockSpec((tm, k), lambda i, j: (i, 0), pipeline_mode=pl.Buffered(1)),
                pl.BlockSpec((tm, LANES), lambda i, j: (i, 0)),
                pl.BlockSpec((pl.Element(tn), pl.Element(k)), lambda i, j: (wt_row(j), 0))]
    args = [xb, rstd, wt]
    body = _proj_kernel
    if col_scale is not None:
        in_specs.append(pl.BlockSpec((1, tn), lambda i, j: (0, j)))
        args.append(col_scale.reshape(1, n).astype(F32))
        body = _proj_scaled_kernel
    return pl.pallas_call(
        body,
        grid=(m // tm, n // tn),
        in_specs=in_specs,
        out_specs=pl.BlockSpec((tm, tn), lambda i, j: (i, j)),
        out_shape=jax.ShapeDtypeStruct((m, n), out_dtype),
        compiler_params=_params(2),
        name="in_project",
    )(*args)


def _merge_kernel(ya_ref, yb_ref, wa_ref, wb_ref, ga_ref, gb_ref, o_ref):
    pa = _dot(ya_ref[...], wa_ref[...].astype(BF16))
    pb = _dot(yb_ref[...], wb_ref[...].astype(BF16))
    o_ref[...] = (jax.nn.sigmoid(ga_ref[...]) * pa
                  + jax.nn.sigmoid(gb_ref[...]) * pb).astype(o_ref.dtype)


def gated_merge(ya, yb, wa, wb, gates, ga_col0, gb_col0):
    m, k = ya.shape
    n = wa.shape[1]
    tm = _pick(m, (1024, 512, 256, 128, 64, 8))
    tn = _pick(n, (512, 256, 128))
    assert ga_col0 % tn == 0 and gb_col0 % tn == 0
    ga_blk, gb_blk = ga_col0 // tn, gb_col0 // tn
    return pl.pallas_call(
        _merge_kernel,
        grid=(m // tm, n // tn),
        in_specs=[pl.BlockSpec((tm, k), lambda i, j: (i, 0)),
                  pl.BlockSpec((tm, k), lambda i, j: (i, 0)),
                  pl.BlockSpec((k, tn), lambda i, j: (0, j)),
                  pl.BlockSpec((k, tn), lambda i, j: (0, j)),
                  pl.BlockSpec((tm, tn), lambda i, j: (i, ga_blk + j)),
                  pl.BlockSpec((tm, tn), lambda i, j: (i, gb_blk + j))],
        out_specs=pl.BlockSpec((tm, tn), lambda i, j: (i, j)),
        out_shape=jax.ShapeDtypeStruct((m, n), BF16),
        compiler_params=_params(2),
        name="gated_merge",
    )(ya, yb, wa, wb, gates, gates)


def _conv_kernel(cur_ref, prev_ref, w_ref, q_ref, k_ref, *, k_scale):
    cur = cur_ref[...]
    prev = jnp.where(pl.program_id(0) > 0, prev_ref[...], 0.0)
    ext = jnp.concatenate([prev, cur], axis=0)
    w = w_ref[...]
    tr = cur.shape[0]
    acc = w[CONV_WIDTH - 1:CONV_WIDTH] * cur
    for j in range(CONV_WIDTH - 1):
        off = 8 - (CONV_WIDTH - 1) + j
        acc = acc + w[j:j + 1] * ext[off:off + tr]
    y = acc * jax.nn.sigmoid(acc)
    half = y.shape[1] // 2
    q_ref[...] = y[:, :half].astype(q_ref.dtype)
    k_ref[...] = (y[:, half:] * k_scale).astype(k_ref.dtype)


def conv_silu_qk(p, width2, conv_w, k_scale):
    s = p.shape[0]
    tr = _pick(s, (256, 128, 64, 8))
    half = width2 // 2
    return pl.pallas_call(
        functools.partial(_conv_kernel, k_scale=k_scale),
        grid=(s // tr,),
        in_specs=[pl.BlockSpec((tr, width2), lambda i: (i, 0)),
                  pl.BlockSpec((8, width2), lambda i: (jnp.maximum(i * (tr // 8) - 1, 0), 0)),
                  pl.BlockSpec((CONV_WIDTH, width2), lambda i: (0, 0))],
        out_specs=[pl.BlockSpec((tr, half), lambda i: (i, 0)),
                   pl.BlockSpec((tr, half), lambda i: (i, 0))],
        out_shape=[jax.ShapeDtypeStruct((s, half), BF16),
                   jax.ShapeDtypeStruct((s, half), BF16)],
        compiler_params=_params(1),
        name="conv_silu_qk",
    )(p, p, conv_w.astype(F32))


def _split3(x):
    hi = x.astype(BF16)
    rem = x - hi.astype(F32)
    mid = rem.astype(BF16)
    lo = (rem - mid.astype(F32)).astype(BF16)
    return jnp.concatenate([hi, mid, lo], axis=1)


def _gate_kernel(g_ref, bi_ref, bf_ref, e_ref, r_ref, brep_ref, rmaxrep_ref, arep_ref, *, chunk):
    g = g_ref[...]
    rows = g.shape[0]
    li = g[:, :LANES] + bi_ref[...]
    lf = _log_sigmoid(g[:, LANES:] + bf_ref[...])
    pos = lax.broadcasted_iota(jnp.int32, lf.shape, 0) % chunk

    def scan(x, op, fill, down):
        d = 1
        while d < chunk:
            if down:
                x = op(x, jnp.where(pos >= d, pltpu.roll(x, d, 0), fill))
            else:
                x = op(x, jnp.where(pos + d < chunk, pltpu.roll(x, rows - d, 0), fill))
            d *= 2
        return x

    b = scan(lf, jnp.add, 0.0, True)
    r = li - b
    rmax = scan(r, jnp.maximum, -jnp.inf, True)
    a = (scan(lf, jnp.add, 0.0, False) - lf) + li
    r_ref[...] = r
    for x, ref in ((b, brep_ref), (rmax, rmaxrep_ref), (a, arep_ref)):
        ref[...] = _dot(_split3(x), e_ref[...])


def gate_prep(g, b_i, b_f, chunk, heads):
    s = g.shape[0]
    tr = _pick(s, (512, 256, 128, 64))
    pad = lambda v: jnp.zeros((1, LANES), F32).at[0, :v.shape[0]].set(v.astype(F32))
    src_lane = lax.broadcasted_iota(jnp.int32, (3 * LANES, heads * LANES), 0) % LANES
    dst_head = lax.broadcasted_iota(jnp.int32, (3 * LANES, heads * LANES), 1) // LANES
    expand = (src_lane == dst_head).astype(BF16)
    rep = jax.ShapeDtypeStruct((s, heads * LANES), F32)
    return pl.pallas_call(
        functools.partial(_gate_kernel, chunk=chunk),
        grid=(s // tr,),
        in_specs=[pl.BlockSpec((tr, 2 * LANES), lambda i: (i, 0)),
                  pl.BlockSpec((1, LANES), lambda i: (0, 0)),
                  pl.BlockSpec((1, LANES), lambda i: (0, 0)),
                  pl.BlockSpec((3 * LANES, heads * LANES), lambda i: (0, 0))],
        out_specs=[pl.BlockSpec((tr, LANES), lambda i: (i, 0))]
                  + [pl.BlockSpec((tr, heads * LANES), lambda i: (i, 0))] * 3,
        out_shape=[jax.ShapeDtypeStruct((s, LANES), F32), rep, rep, rep],
        compiler_params=_params(1),
        name="gate_prep",
    )(g, pad(b_i), pad(b_f), expand)


def _mlstm_kernel(q_ref, k_ref, v_ref, mo_ref, brep_ref, rmaxrep_ref, arep_ref, r_ref, g_ref, o_ref,
                  cn_ref, m_ref, *, heads, dk, dv, chunk, n_chunks):
    @pl.when(pl.program_id(0) == 0)
    def _():
        cn_ref[...] = jnp.zeros_like(cn_ref)
        m_ref[...] = jnp.zeros_like(m_ref)

    row = lax.broadcasted_iota(jnp.int32, (chunk, chunk), 0)
    col = lax.broadcasted_iota(jnp.int32, (chunk, chunk), 1)
    tril = row >= col
    ones_ext = jnp.ones((chunk, LANES), BF16)
    ones_sum = jnp.ones((2 * dv, LANES), BF16)
    v_tiles = dv // LANES
    rep = _rep

    def lanes(h):
        return slice(h * LANES, (h + 1) * LANES)

    def chunk_body(c, carry):
        r0 = pl.multiple_of(c * chunk, chunk)
        rows = pl.ds(r0, chunk)
        tail = pl.ds(pl.multiple_of(r0 + chunk - 8, 8), 8)
        r_all = r_ref[c]
        hs = range(heads)
        q = [q_ref[rows, h * dk:(h + 1) * dk] for h in hs]
        k = [k_ref[rows, h * dk:(h + 1) * dk] for h in hs]
        v_ext = [jnp.concatenate([v_ref[rows, h * dv:(h + 1) * dv], ones_ext], axis=1) for h in hs]
        cn = [cn_ref[h] for h in hs]
        s = [_dot_nt(q[h], k[h]) for h in hs]
        inter = [_dot(q[h], cn[h].astype(BF16)) for h in hs]

        m_prev = [m_ref[h:h + 1, :] for h in hs]
        big_m = [jnp.maximum(m_prev[h], rmaxrep_ref[rows, lanes(h)]) for h in hs]
        w = []
        for h in hs:
            d_exp = jnp.exp(r_all[h:h + 1, :] - big_m[h][:, :chunk])
            w.append((s[h] * jnp.where(tril, d_exp, 0.0)).astype(BF16))
        intra = [_dot(w[h], v_ext[h]) for h in hs]

        for h in hs:
            decay = jnp.exp(m_prev[h] - big_m[h])
            tot = rep(decay, v_tiles + 1) * inter[h] + intra[h]
            floor = jnp.exp(-(brep_ref[rows, lanes(h)] + big_m[h]))
            denom = jnp.maximum(jnp.abs(tot[:, dv:]), floor)
            hh = tot[:, :dv] / rep(denom, v_tiles)
            sq = hh * hh
            sq_hi = sq.astype(BF16)
            sq_lo = (sq - sq_hi.astype(F32)).astype(BF16)
            ms = _dot(jnp.concatenate([sq_hi, sq_lo], axis=1), ones_sum) * (1.0 / dv)
            y = hh * rep(lax.rsqrt(ms + EPS), v_tiles) * g_ref[:, h * dv:(h + 1) * dv]
            y = y * jax.nn.sigmoid(mo_ref[rows, h * dv:(h + 1) * dv])
            o_ref[rows, h * dv:(h + 1) * dv] = y.astype(o_ref.dtype)

        sv, m_new, carry_scale = [], [], []
        for h in hs:
            a_rep = arep_ref[rows, lanes(h)]
            b_last = brep_ref[tail, lanes(h)][7:8, :]
            m_new.append(jnp.maximum(b_last + m_prev[h], jnp.max(a_rep, axis=0, keepdims=True)))
            carry_scale.append(jnp.exp(b_last + m_prev[h] - m_new[h]))
            src = jnp.exp(a_rep - m_new[h])
            sv.append((rep(src, v_tiles + 1) * v_ext[h].astype(F32)).astype(BF16))
        upd = [lax.dot_general(k[h], sv[h], (((0,), (0,)), ((), ())), preferred_element_type=F32)
               for h in hs]
        for h in hs:
            cn_ref[h] = rep(carry_scale[h], v_tiles + 1) * cn[h] + upd[h]
            m_ref[h:h + 1, :] = m_new[h]
        return carry

    lax.fori_loop(0, n_chunks, chunk_body, 0)


def mlstm(q, k, v, v_col0, mo, mo_col0, b_rep, rmax_rep, a_rep, r_rows, g_out, heads, dk, dv, chunk):
    s = q.shape[0]
    rows = _pick(s, (512, 256, 128, 64))
    n_chunks = rows // chunk
    wv = heads * dv
    assert v_col0 % wv == 0 and mo_col0 % wv == 0 and dv % LANES == 0
    v_blk, mo_blk = v_col0 // wv, mo_col0 // wv
    rep_spec = pl.BlockSpec((rows, heads * LANES), lambda i: (i, 0))
    return pl.pallas_call(
        functools.partial(_mlstm_kernel, heads=heads, dk=dk, dv=dv, chunk=chunk, n_chunks=n_chunks),
        grid=(s // rows,),
        in_specs=[pl.BlockSpec((rows, heads * dk), lambda i: (i, 0)),
                  pl.BlockSpec((rows, heads * dk), lambda i: (i, 0)),
                  pl.BlockSpec((rows, wv), lambda i: (i, v_blk)),
                  pl.BlockSpec((rows, wv), lambda i: (i, mo_blk)),
                  rep_spec, rep_spec, rep_spec,
                  pl.BlockSpec((n_chunks, heads, chunk), lambda i: (i, 0, 0)),
                  pl.BlockSpec((1, wv), lambda i: (0, 0))],
        out_specs=pl.BlockSpec((rows, wv), lambda i: (i, 0)),
        out_shape=jax.ShapeDtypeStruct((s, wv), BF16),
        scratch_shapes=[pltpu.VMEM((heads, dk, dv + LANES), F32),
                        pltpu.VMEM((heads, LANES), F32)],
        compiler_params=_params(1),
        name="mlstm",
    )(q, k, v, mo, b_rep, rmax_rep, a_rep, r_rows, g_out.reshape(1, wv).astype(F32))


def _sb_kernel(q_ref, k_ref, v_ref, o_ref, acc_ref, run_ref, live_ref, *, tq, tk, hp, d):
    i = pl.program_id(1)
    groups = tq // tk
    row = lax.broadcasted_iota(jnp.int32, (tk, tk), 0)
    col = lax.broadcasted_iota(jnp.int32, (tk, tk), 1)
    diag_strict = col < row
    sums_rhs = jnp.concatenate([(row > col).astype(BF16), jnp.ones((tk, tk), BF16)], axis=1)
    sums_rhs = jnp.concatenate([sums_rhs, sums_rhs], axis=0)

    def walk(tiles):
        zs = [_dot_nt(q, kk) for q, kk, _, _, _ in tiles]
        log_betas, splits = [], []
        for z, (_, _, _, _, mask) in zip(zs, tiles):
            log_beta = jnp.minimum(z, 0.0) - jnp.log(1.0 + jnp.exp(-jnp.abs(z)))
            log_rest = log_beta - z
            if mask is not None:
                log_rest = jnp.where(mask, log_rest, 0.0)
            hi = log_rest.astype(BF16)
            lo = (log_rest - hi.astype(F32)).astype(BF16)
            log_betas.append(log_beta)
            splits.append(jnp.concatenate([hi, lo], axis=1))
        sums = [_dot(x, sums_rhs) for x in splits]
        weights, runs = [], []
        for log_beta, sm, (_, _, _, run, mask) in zip(log_betas, sums, tiles):
            log_w = log_beta + sm[:, :tk]
            wts = jnp.exp(log_w if run is None else log_w + run)
            if mask is not None:
                wts = jnp.where(mask, wts, 0.0)
            weights.append(wts.astype(BF16))
            runs.append(sm[:, tk:] if run is None else run + sm[:, tk:])
        outs = [_dot(w, vv) for w, (_, _, vv, _, _) in zip(weights, tiles)]
        return list(zip(outs, runs))

    def key_rows(blk):
        return pl.ds(pl.multiple_of(blk * tk, tk), tk)

    def cols(h):
        return slice(h * d, (h + 1) * d)

    def rows(g):
        return slice(g * tk, (g + 1) * tk)

    acc = {}
    run = {}
    for r in range(groups):
        ids = [(h, g) for h in range(hp) for g in range(r, groups)]
        tiles = []
        for h, g in ids:
            kr = key_rows(i * groups + g - r)
            tiles.append((q_ref[rows(g), cols(h)], k_ref[kr, cols(h)], v_ref[kr, cols(h)],
                          run.get((h, g)), diag_strict if r == 0 else None))
        for key, (out, new_run) in zip(ids, walk(tiles)):
            acc[key] = out if r == 0 else acc[key] + out
            run[key] = new_run
    for (h, g), value in acc.items():
        acc_ref[h, rows(g), :] = value
        run_ref[h, rows(g), :] = run[(h, g)]
    live_ref[0] = (jnp.max(run_ref[...]) > SB_DONE_LOG).astype(jnp.int32)

    n_prev = i * groups

    def block_body(step, carry):
        @pl.when(live_ref[0] > 0)
        def _():
            kr = key_rows(n_prev - 1 - step)
            tiles = [(q_ref[:, cols(h)], k_ref[kr, cols(h)], v_ref[kr, cols(h)], run_ref[h], None)
                     for h in range(hp)]
            for h, (out, new_run) in enumerate(walk(tiles)):
                acc_ref[h] += out
                run_ref[h] = new_run
            live_ref[0] = (jnp.max(run_ref[...]) > SB_DONE_LOG).astype(jnp.int32)
        return carry

    lax.fori_loop(0, n_prev, block_body, 0)
    for h in range(hp):
        o_ref[:, cols(h)] = acc_ref[h].astype(o_ref.dtype)


def stick_breaking(qkv, col0, heads, d):
    s = qkv.shape[0]
    tq = tk = SB_Q_ROWS
    hp = SB_HEADS_PER_STEP
    assert s % tq == 0 and heads % hp == 0 and col0 % (hp * d) == 0
    n_hg = heads // hp
    blk0 = col0 // (hp * d)
    once = pl.Buffered(1)
    return pl.pallas_call(
        functools.partial(_sb_kernel, tq=tq, tk=tk, hp=hp, d=d),
        grid=(n_hg, s // tq),
        in_specs=[pl.BlockSpec((tq, hp * d), lambda hg, i: (i, blk0 + hg)),
                  pl.BlockSpec((s, hp * d), lambda hg, i: (0, blk0 + n_hg + hg), pipeline_mode=once),
                  pl.BlockSpec((s, hp * d), lambda hg, i: (0, blk0 + 2 * n_hg + hg), pipeline_mode=once)],
        out_specs=pl.BlockSpec((tq, hp * d), lambda hg, i: (i, hg)),
        out_shape=jax.ShapeDtypeStruct((s, heads * d), BF16),
        scratch_shapes=[pltpu.VMEM((hp, tq, d), F32),
                        pltpu.VMEM((hp, tq, tk), F32),
                        pltpu.SMEM((1,), jnp.int32)],
        compiler_params=_params(2),
        name="stick_breaking",
    )(qkv, qkv, qkv)


def _ffn(x, xb, rstd, w1, w3, w2, next_gain):
    u, w2b = gateup(xb, rstd, w1, w3, w2)
    return matmul_residual(u, w2b, x, 0.5, tm_max=512, tn_max=512, next_gain=next_gain)


def _mixers(x, xb, rstd, w_in, conv_qk, b_igate, b_fgate, g_mlstm_out, w_proj_a, w_proj_b, w_out, next_gain):
    s, d_model = x.shape
    qk_w = ML_HEADS * ML_QK_DIM
    ml_w = ML_HEADS * ML_V_DIM
    sb_w = SB_HEADS * SB_HEAD_DIM
    o_mv = 2 * qk_w
    o_mo = o_mv + ml_w
    o_mi = o_mo + ml_w
    o_mf = o_mi + ML_HEADS
    o_sq = o_mf + ML_HEADS
    o_ga = o_sq + 3 * sb_w
    wt = jnp.swapaxes(w_in, 0, 1)
    zrows = jnp.zeros((LANES - ML_HEADS, d_model), F32)
    wt_gate = jnp.concatenate([wt[o_mi:o_mf], zrows, wt[o_mf:o_sq], zrows], axis=0)
    scale16 = jnp.concatenate([jnp.ones((ml_w,), F32),
                               jnp.full((sb_w,), SB_HEAD_DIM ** -0.5, F32),
                               jnp.ones((2 * sb_w,), F32)])

    p32 = project(xb, rstd, wt, [(0, o_mv), (o_mo, ml_w), (o_ga, 2 * d_model)], F32)
    p16 = project(xb, rstd, wt, [(o_mv, ml_w), (o_sq, 3 * sb_w)], BF16, col_scale=scale16)
    pg = project(xb, rstd, wt_gate, [(0, 2 * LANES)], F32)

    q, k = conv_silu_qk(p32, 2 * qk_w, conv_qk, ML_QK_DIM ** -0.5)
    r, b_rep, rmax_rep, a_rep = gate_prep(pg, b_igate, b_fgate, ML_CHUNK, ML_HEADS)
    r_rows = jnp.swapaxes(r[:, :ML_HEADS].reshape(s // ML_CHUNK, ML_CHUNK, ML_HEADS), 1, 2)
    ya = mlstm(q, k, p16, 0, p32, 2 * qk_w, b_rep, rmax_rep, a_rep, r_rows, g_mlstm_out,
               ML_HEADS, ML_QK_DIM, ML_V_DIM, ML_CHUNK)

    yb = stick_breaking(p16, ml_w, SB_HEADS, SB_HEAD_DIM)

    merged = gated_merge(ya, yb, w_proj_a, w_proj_b, p32, 2 * qk_w + ml_w, 2 * qk_w + ml_w + d_model)
    return matmul_residual(merged, w_out, x, 1.0, tm_max=1024, tn_max=512, next_gain=next_gain)


def kernel(x, g_ffn1, w1_ffn1, w3_ffn1, w2_ffn1, g_mix, w_in, conv_qk, b_igate, b_fgate,
           g_mlstm_out, w_proj_a, w_proj_b, w_out, g_ffn2, w1_ffn2, w3_ffn2, w2_ffn2, g_final):
    batch, seq, d_model = x.shape
    outs = []
    depth = g_ffn1.shape[0]
    for bi in range(batch):
        xf = x[bi]
        xb, rstd = cast_stats(xf, g_ffn1[0])
        for l in range(depth):
            xf, xb, rstd = _ffn(xf, xb, rstd, w1_ffn1[l], w3_ffn1[l], w2_ffn1[l], g_mix[l])
            xf, xb, rstd = _mixers(xf, xb, rstd, w_in[l], conv_qk[l], b_igate[l], b_fgate[l],
                                   g_mlstm_out[l], w_proj_a[l], w_proj_b[l], w_out[l], g_ffn2[l])
            if l + 1 < depth:
                xf, xb, rstd = _ffn(xf, xb, rstd, w1_ffn2[l], w3_ffn2[l], w2_ffn2[l], g_ffn1[l + 1])
            else:
                xf = _ffn(xf, xb, rstd, w1_ffn2[l], w3_ffn2[l], w2_ffn2[l], None)
        outs.append(rmsnorm(xf, g_final, x.dtype))
    return jnp.stack(outs, axis=0)
```

```python
import functools

import jax
import jax.numpy as jnp
from jax import lax
from jax.experimental import pallas as pl
from jax.experimental.pallas import tpu as pltpu

F32 = jnp.float32
BF16 = jnp.bfloat16
EPS = 1e-6

V7X_VMEM_BYTES = 64 * 1024 * 1024
VMEM_LIMIT_BYTES = V7X_VMEM_BYTES - 8 * 1024 * 1024
LANES = 128
BF16_SUBLANES = 16

ML_HEADS = 8
ML_QK_DIM = 128
ML_V_DIM = 256
ML_CHUNK = 64
SB_HEADS = 16
SB_HEAD_DIM = 128
EPILOGUE_ROWS = 256
SB_HEADS_PER_STEP = 8
SB_Q_ROWS = 128
SB_STATIC_BLOCKS = 3
CONV_WIDTH = 4
SB_DONE_LOG = -88.0


def _params(n_grid):
    return pltpu.CompilerParams(
        dimension_semantics=("arbitrary",) * n_grid,
        vmem_limit_bytes=VMEM_LIMIT_BYTES)


def _pick(n, candidates):
    for c in candidates:
        if n % c == 0:
            return c
    raise ValueError(f"no tile in {candidates} divides {n}")


def _log_sigmoid(x):
    return jnp.minimum(x, 0.0) - jnp.log1p(jnp.exp(-jnp.abs(x)))


def _dot(a, b):
    return jnp.dot(a, b, preferred_element_type=F32)


def _dot_nt(a, b):
    return lax.dot_general(a, b, (((1,), (1,)), ((), ())), preferred_element_type=F32)


def _rep(x, n):
    return x if n == 1 else jnp.concatenate([x] * n, axis=1)


def _rmsnorm_kernel(x_ref, g_ref, o_ref):
    x = x_ref[...]
    ms = jnp.mean(x * x, axis=-1, keepdims=True)
    o_ref[...] = (x * lax.rsqrt(ms + EPS) * g_ref[...]).astype(o_ref.dtype)


def rmsnorm(x, g, out_dtype):
    s, d = x.shape
    tr = _pick(s, (256, 128, 64, 8))
    return pl.pallas_call(
        _rmsnorm_kernel,
        grid=(s // tr,),
        in_specs=[pl.BlockSpec((tr, d), lambda i: (i, 0)),
                  pl.BlockSpec((1, d), lambda i: (0, 0))],
        out_specs=pl.BlockSpec((tr, d), lambda i: (i, 0)),
        out_shape=jax.ShapeDtypeStruct((s, d), out_dtype),
        compiler_params=_params(1),
        name="rmsnorm",
    )(x, g.reshape(1, d).astype(F32))


def _cast_stats_kernel(x_ref, g_ref, xb_ref, rstd_ref):
    x = x_ref[...]
    xb_ref[...] = (x * g_ref[...]).astype(BF16)
    ms = jnp.mean(x * x, axis=-1, keepdims=True)
    rstd_ref[...] = jnp.broadcast_to(lax.rsqrt(ms + EPS), rstd_ref.shape)


def cast_stats(x, gain):
    s, d = x.shape
    tr = _pick(s, (256, 128, 64, 8))
    return pl.pallas_call(
        _cast_stats_kernel,
        grid=(s // tr,),
        in_specs=[pl.BlockSpec((tr, d), lambda i: (i, 0)),
                  pl.BlockSpec((1, d), lambda i: (0, 0))],
        out_specs=[pl.BlockSpec((tr, d), lambda i: (i, 0)),
                   pl.BlockSpec((tr, LANES), lambda i: (i, 0))],
        out_shape=[jax.ShapeDtypeStruct((s, d), BF16),
                   jax.ShapeDtypeStruct((s, LANES), F32)],
        compiler_params=_params(1),
        name="cast_stats",
    )(x, gain.reshape(1, d).astype(F32))


def _row_chunks(n_rows):
    rows = min(EPILOGUE_ROWS, n_rows)
    assert n_rows % rows == 0
    return [slice(r, r + rows) for r in range(0, n_rows, rows)]


def _gateup_kernel(a_ref, rstd_ref, w1_ref, w3_ref, w2_ref, o_ref, w2b_ref):
    w1 = w1_ref[...].astype(BF16)
    w3 = w3_ref[...].astype(BF16)
    for sl in _row_chunks(o_ref.shape[0]):
        a = a_ref[sl, :]
        rstd = _rep(rstd_ref[sl, :], o_ref.shape[1] // LANES)
        g = rstd * _dot(a, w1)
        u = rstd * _dot(a, w3)
        o_ref[sl, :] = (g * jax.nn.sigmoid(g) * u).astype(o_ref.dtype)
    w2b_ref[...] = w2_ref[...].astype(BF16)


def gateup(xb, rstd, w1, w3, w2):
    m, k = xb.shape
    n = w1.shape[1]
    k2, n2 = w2.shape
    tm = _pick(m, (2048, 1024, 512, 256, 128, 64, 8))
    tn = _pick(n, (256, 128))
    n_i, n_j = m // tm, n // tn
    slab = k2 // (n_i * n_j)
    assert slab * n_i * n_j == k2 and slab % BF16_SUBLANES == 0, (k2, n_i, n_j)
    return pl.pallas_call(
        _gateup_kernel,
        grid=(n_i, n_j),
        in_specs=[pl.BlockSpec((tm, k), lambda i, j: (i, 0), pipeline_mode=pl.Buffered(1)),
                  pl.BlockSpec((tm, LANES), lambda i, j: (i, 0)),
                  pl.BlockSpec((k, tn), lambda i, j: (0, j)),
                  pl.BlockSpec((k, tn), lambda i, j: (0, j)),
                  pl.BlockSpec((slab, n2), lambda i, j: (i * n_j + j, 0))],
        out_specs=[pl.BlockSpec((tm, tn), lambda i, j: (i, j)),
                   pl.BlockSpec((slab, n2), lambda i, j: (i * n_j + j, 0))],
        out_shape=[jax.ShapeDtypeStruct((m, n), BF16),
                   jax.ShapeDtypeStruct((k2, n2), BF16)],
        compiler_params=_params(2),
        name="ffn_gateup",
    )(xb, rstd, w1, w3, w2)


def _resid_kernel(a_ref, w_ref, x_ref, o_ref, *, alpha):
    w = w_ref[...].astype(BF16)
    for sl in _row_chunks(o_ref.shape[0]):
        o_ref[sl, :] = x_ref[sl, :] + alpha * _dot(a_ref[sl, :], w)


def _resid_stats_kernel(a_ref, w_ref, x_ref, g_ref, o_ref, ob_ref, rstd_ref, ssq_ref, *, alpha, n_total):
    j = pl.program_id(1)
    w = w_ref[...].astype(BF16)
    parts = []
    for sl in _row_chunks(o_ref.shape[0]):
        y = x_ref[sl, :] + alpha * _dot(a_ref[sl, :], w)
        o_ref[sl, :] = y
        ob_ref[sl, :] = (y * g_ref[...]).astype(BF16)
        sq = y * y
        part = sq[:, :LANES]
        for t in range(1, sq.shape[1] // LANES):
            part = part + sq[:, t * LANES:(t + 1) * LANES]
        parts.append(part)
    part = jnp.concatenate(parts, axis=0)

    @pl.when(j == 0)
    def _():
        ssq_ref[...] = part

    @pl.when(j > 0)
    def _():
        ssq_ref[...] += part

    @pl.when(j == pl.num_programs(1) - 1)
    def _():
        ms = jnp.sum(ssq_ref[...], axis=-1, keepdims=True) * (1.0 / n_total)
        rstd_ref[...] = jnp.broadcast_to(lax.rsqrt(ms + EPS), rstd_ref.shape)


def matmul_residual(a, w, x, alpha, tm_max, tn_max, next_gain=None):
    m, k = a.shape
    n = w.shape[1]
    tm = _pick(m, tuple(t for t in (1024, 512, 256, 128, 64, 8) if t <= tm_max))
    tn = _pick(n, tuple(t for t in (512, 256, 128) if t <= tn_max))
    tile = pl.BlockSpec((tm, tn), lambda i, j: (i, j))
    in_specs = [pl.BlockSpec((tm, k), lambda i, j: (i, 0)),
                pl.BlockSpec((k, tn), lambda i, j: (0, j)),
                tile]
    if next_gain is None:
        return pl.pallas_call(
            functools.partial(_resid_kernel, alpha=alpha),
            grid=(m // tm, n // tn),
            in_specs=in_specs,
            out_specs=tile,
            out_shape=jax.ShapeDtypeStruct((m, n), F32),
            compiler_params=_params(2),
            name="matmul_residual",
        )(a, w, x)
    return pl.pallas_call(
        functools.partial(_resid_stats_kernel, alpha=alpha, n_total=n),
        grid=(m // tm, n // tn),
        in_specs=in_specs + [pl.BlockSpec((1, tn), lambda i, j: (0, j))],
        out_specs=[tile, tile, pl.BlockSpec((tm, LANES), lambda i, j: (i, 0))],
        out_shape=[jax.ShapeDtypeStruct((m, n), F32),
                   jax.ShapeDtypeStruct((m, n), BF16),
                   jax.ShapeDtypeStruct((m, LANES), F32)],
        scratch_shapes=[pltpu.VMEM((tm, LANES), F32)],
        compiler_params=_params(2),
        name="matmul_residual_stats",
    )(a, w, x, next_gain.reshape(1, n).astype(F32))


def _proj_kernel(a_ref, rstd_ref, wt_ref, o_ref):
    wt = wt_ref[...].astype(BF16)
    for sl in _row_chunks(o_ref.shape[0]):
        y = _dot_nt(a_ref[sl, :], wt)
        o_ref[sl, :] = (_rep(rstd_ref[sl, :], y.shape[1] // LANES) * y).astype(o_ref.dtype)


def _proj_scaled_kernel(a_ref, rstd_ref, wt_ref, s_ref, o_ref):
    wt = wt_ref[...].astype(BF16)
    for sl in _row_chunks(o_ref.shape[0]):
        y = _dot_nt(a_ref[sl, :], wt)
        o_ref[sl, :] = (_rep(rstd_ref[sl, :], y.shape[1] // LANES) * y * s_ref[...]).astype(o_ref.dtype)


def project(xb, rstd, wt, segments, out_dtype, col_scale=None):
    m, k = xb.shape
    n = sum(seg_n for _, seg_n in segments)
    tm = _pick(m, (2048, 1024, 512, 256, 128, 64, 8))
    tn = 512 if all(seg_n % 512 == 0 for _, seg_n in segments) else 256
    assert all(seg_n % tn == 0 and row0 % 8 == 0 for row0, seg_n in segments)

    def wt_row(j):
        blk0 = 0
        row8 = None
        for row0, seg_n in segments:
            here = row0 // 8 + (j - blk0) * (tn // 8)
            row8 = here if row8 is None else jnp.where(j >= blk0, here, row8)
            blk0 += seg_n // tn
        return row8 * 8

    in_specs = [pl.BlockSpec((tm, k), lambda i, j: (i, 0), pipeline_mode=pl.Buffered(1)),
                pl.BlockSpec((tm, LANES), lambda i, j: (i, 0)),
                pl.BlockSpec((pl.Element(tn), pl.Element(k)), lambda i, j: (wt_row(j), 0))]
    args = [xb, rstd, wt]
    body = _proj_kernel
    if col_scale is not None:
        in_specs.append(pl.BlockSpec((1, tn), lambda i, j: (0, j)))
        args.append(col_scale.reshape(1, n).astype(F32))
        body = _proj_scaled_kernel
    return pl.pallas_call(
        body,
        grid=(m // tm, n // tn),
        in_specs=in_specs,
        out_specs=pl.BlockSpec((tm, tn), lambda i, j: (i, j)),
        out_shape=jax.ShapeDtypeStruct((m, n), out_dtype),
        compiler_params=_params(2),
        name="in_project",
    )(*args)


def _merge_kernel(ya_ref, yb_ref, wa_ref, wb_ref, ga_ref, gb_ref, o_ref):
    wa = wa_ref[...].astype(BF16)
    wb = wb_ref[...].astype(BF16)
    for sl in _row_chunks(o_ref.shape[0]):
        pa = _dot(ya_ref[sl, :], wa)
        pb = _dot(yb_ref[sl, :], wb)
        o_ref[sl, :] = (jax.nn.sigmoid(ga_ref[sl, :]) * pa
                        + jax.nn.sigmoid(gb_ref[sl, :]) * pb).astype(o_ref.dtype)


def gated_merge(ya, yb, wa, wb, gates, ga_col0, gb_col0):
    m, k = ya.shape
    n = wa.shape[1]
    tm = _pick(m, (1024, 512, 256, 128, 64, 8))
    tn = _pick(n, (512, 256, 128))
    assert ga_col0 % tn == 0 and gb_col0 % tn == 0
    ga_blk, gb_blk = ga_col0 // tn, gb_col0 // tn
    return pl.pallas_call(
        _merge_kernel,
        grid=(m // tm, n // tn),
        in_specs=[pl.BlockSpec((tm, k), lambda i, j: (i, 0)),
                  pl.BlockSpec((tm, k), lambda i, j: (i, 0)),
                  pl.BlockSpec((k, tn), lambda i, j: (0, j)),
                  pl.BlockSpec((k, tn), lambda i, j: (0, j)),
                  pl.BlockSpec((tm, tn), lambda i, j: (i, ga_blk + j)),
                  pl.BlockSpec((tm, tn), lambda i, j: (i, gb_blk + j))],
        out_specs=pl.BlockSpec((tm, tn), lambda i, j: (i, j)),
        out_shape=jax.ShapeDtypeStruct((m, n), BF16),
        compiler_params=_params(2),
        name="gated_merge",
    )(ya, yb, wa, wb, gates, gates)


def _conv_kernel(cur_ref, prev_ref, w_ref, q_ref, k_ref, *, k_scale):
    cur = cur_ref[...]
    prev = jnp.where(pl.program_id(0) > 0, prev_ref[...], 0.0)
    ext = jnp.concatenate([prev, cur], axis=0)
    w = w_ref[...]
    tr = cur.shape[0]
    acc = w[CONV_WIDTH - 1:CONV_WIDTH] * cur
    for j in range(CONV_WIDTH - 1):
        off = 8 - (CONV_WIDTH - 1) + j
        acc = acc + w[j:j + 1] * ext[off:off + tr]
    y = acc * jax.nn.sigmoid(acc)
    half = y.shape[1] // 2
    q_ref[...] = y[:, :half].astype(q_ref.dtype)
    k_ref[...] = (y[:, half:] * k_scale).astype(k_ref.dtype)


def conv_silu_qk(p, width2, conv_w, k_scale):
    s = p.shape[0]
    tr = _pick(s, (256, 128, 64, 8))
    half = width2 // 2
    return pl.pallas_call(
        functools.partial(_conv_kernel, k_scale=k_scale),
        grid=(s // tr,),
        in_specs=[pl.BlockSpec((tr, width2), lambda i: (i, 0)),
                  pl.BlockSpec((8, width2), lambda i: (jnp.maximum(i * (tr // 8) - 1, 0), 0)),
                  pl.BlockSpec((CONV_WIDTH, width2), lambda i: (0, 0))],
        out_specs=[pl.BlockSpec((tr, half), lambda i: (i, 0)),
                   pl.BlockSpec((tr, half), lambda i: (i, 0))],
        out_shape=[jax.ShapeDtypeStruct((s, half), BF16),
                   jax.ShapeDtypeStruct((s, half), BF16)],
        compiler_params=_params(1),
        name="conv_silu_qk",
    )(p, p, conv_w.astype(F32))


def _split3(x):
    hi = x.astype(BF16)
    rem = x - hi.astype(F32)
    mid = rem.astype(BF16)
    lo = (rem - mid.astype(F32)).astype(BF16)
    return jnp.concatenate([hi, mid, lo], axis=1)


def _gate_kernel(g_ref, bi_ref, bf_ref, e_ref, r_ref, brep_ref, rmaxrep_ref, arep_ref, *, chunk):
    g = g_ref[...]
    rows = g.shape[0]
    li = g[:, :LANES] + bi_ref[...]
    lf = _log_sigmoid(g[:, LANES:] + bf_ref[...])
    pos = lax.broadcasted_iota(jnp.int32, lf.shape, 0) % chunk

    def scan(x, op, fill, down):
        d = 1
        while d < chunk:
            if down:
                x = op(x, jnp.where(pos >= d, pltpu.roll(x, d, 0), fill))
            else:
                x = op(x, jnp.where(pos + d < chunk, pltpu.roll(x, rows - d, 0), fill))
            d *= 2
        return x

    b = scan(lf, jnp.add, 0.0, True)
    r = li - b
    rmax = scan(r, jnp.maximum, -jnp.inf, True)
    a = (scan(lf, jnp.add, 0.0, False) - lf) + li
    r_ref[...] = r
    for x, ref in ((b, brep_ref), (rmax, rmaxrep_ref), (a, arep_ref)):
        ref[...] = _dot(_split3(x), e_ref[...])


def gate_prep(g, b_i, b_f, chunk, heads):
    s = g.shape[0]
    tr = _pick(s, (512, 256, 128, 64))
    pad = lambda v: jnp.zeros((1, LANES), F32).at[0, :v.shape[0]].set(v.astype(F32))
    src_lane = lax.broadcasted_iota(jnp.int32, (3 * LANES, heads * LANES), 0) % LANES
    dst_head = lax.broadcasted_iota(jnp.int32, (3 * LANES, heads * LANES), 1) // LANES
    expand = (src_lane == dst_head).astype(BF16)
    rep = jax.ShapeDtypeStruct((s, heads * LANES), F32)
    return pl.pallas_call(
        functools.partial(_gate_kernel, chunk=chunk),
        grid=(s // tr,),
        in_specs=[pl.BlockSpec((tr, 2 * LANES), lambda i: (i, 0)),
                  pl.BlockSpec((1, LANES), lambda i: (0, 0)),
                  pl.BlockSpec((1, LANES), lambda i: (0, 0)),
                  pl.BlockSpec((3 * LANES, heads * LANES), lambda i: (0, 0))],
        out_specs=[pl.BlockSpec((tr, LANES), lambda i: (i, 0))]
                  + [pl.BlockSpec((tr, heads * LANES), lambda i: (i, 0))] * 3,
        out_shape=[jax.ShapeDtypeStruct((s, LANES), F32), rep, rep, rep],
        compiler_params=_params(1),
        name="gate_prep",
    )(g, pad(b_i), pad(b_f), expand)


def _mlstm_kernel(q_ref, k_ref, v_ref, mo_ref, brep_ref, rmaxrep_ref, arep_ref, r_ref, g_ref, o_ref,
                  cn_ref, m_ref, *, heads, dk, dv, chunk, n_chunks):
    @pl.when(pl.program_id(0) == 0)
    def _():
        cn_ref[...] = jnp.zeros_like(cn_ref)
        m_ref[...] = jnp.zeros_like(m_ref)

    row = lax.broadcasted_iota(jnp.int32, (chunk, chunk), 0)
    col = lax.broadcasted_iota(jnp.int32, (chunk, chunk), 1)
    tril = row >= col
    ones_ext = jnp.ones((chunk, LANES), BF16)
    ones_sum = jnp.ones((2 * dv, LANES), BF16)
    v_tiles = dv // LANES
    rep = _rep

    def lanes(h):
        return slice(h * LANES, (h + 1) * LANES)

    def chunk_body(c, carry):
        r0 = pl.multiple_of(c * chunk, chunk)
        rows = pl.ds(r0, chunk)
        tail = pl.ds(pl.multiple_of(r0 + chunk - 8, 8), 8)
        r_all = r_ref[c]
        hs = range(heads)
        q = [q_ref[rows, h * dk:(h + 1) * dk] for h in hs]
        k = [k_ref[rows, h * dk:(h + 1) * dk] for h in hs]
        v_ext = [jnp.concatenate([v_ref[rows, h * dv:(h + 1) * dv], ones_ext], axis=1) for h in hs]
        cn = [cn_ref[h] for h in hs]
        s = [_dot_nt(q[h], k[h]) for h in hs]
        inter = [_dot(q[h], cn[h].astype(BF16)) for h in hs]

        m_prev = [m_ref[h:h + 1, :] for h in hs]
        big_m = [jnp.maximum(m_prev[h], rmaxrep_ref[rows, lanes(h)]) for h in hs]
        w = []
        for h in hs:
            d_exp = jnp.exp(r_all[h:h + 1, :] - big_m[h][:, :chunk])
            w.append((s[h] * jnp.where(tril, d_exp, 0.0)).astype(BF16))
        intra = [_dot(w[h], v_ext[h]) for h in hs]

        for h in hs:
            decay = jnp.exp(m_prev[h] - big_m[h])
            tot = rep(decay, v_tiles + 1) * inter[h] + intra[h]
            floor = jnp.exp(-(brep_ref[rows, lanes(h)] + big_m[h]))
            denom = jnp.maximum(jnp.abs(tot[:, dv:]), floor)
            hh = tot[:, :dv] / rep(denom, v_tiles)
            sq = hh * hh
            sq_hi = sq.astype(BF16)
            sq_lo = (sq - sq_hi.astype(F32)).astype(BF16)
            ms = _dot(jnp.concatenate([sq_hi, sq_lo], axis=1), ones_sum) * (1.0 / dv)
            y = hh * rep(lax.rsqrt(ms + EPS), v_tiles) * g_ref[:, h * dv:(h + 1) * dv]
            y = y * jax.nn.sigmoid(mo_ref[rows, h * dv:(h + 1) * dv])
            o_ref[rows, h * dv:(h + 1) * dv] = y.astype(o_ref.dtype)

        sv, m_new, carry_scale = [], [], []
        for h in hs:
            a_rep = arep_ref[rows, lanes(h)]
            b_last = brep_ref[tail, lanes(h)][7:8, :]
            m_new.append(jnp.maximum(b_last + m_prev[h], jnp.max(a_rep, axis=0, keepdims=True)))
            carry_scale.append(jnp.exp(b_last + m_prev[h] - m_new[h]))
            src = jnp.exp(a_rep - m_new[h])
            sv.append((rep(src, v_tiles + 1) * v_ext[h].astype(F32)).astype(BF16))
        upd = [lax.dot_general(k[h], sv[h], (((0,), (0,)), ((), ())), preferred_element_type=F32)
               for h in hs]
        for h in hs:
            cn_ref[h] = rep(carry_scale[h], v_tiles + 1) * cn[h] + upd[h]
            m_ref[h:h + 1, :] = m_new[h]
        return carry

    lax.fori_loop(0, n_chunks, chunk_body, 0)


def mlstm(q, k, v, v_col0, mo, mo_col0, b_rep, rmax_rep, a_rep, r_rows, g_out, heads, dk, dv, chunk):
    s = q.shape[0]
    rows = _pick(s, (512, 256, 128, 64))
    n_chunks = rows // chunk
    wv = heads * dv
    assert v_col0 % wv == 0 and mo_col0 % wv == 0 and dv % LANES == 0
    v_blk, mo_blk = v_col0 // wv, mo_col0 // wv
    rep_spec = pl.BlockSpec((rows, heads * LANES), lambda i: (i, 0))
    return pl.pallas_call(
        functools.partial(_mlstm_kernel, heads=heads, dk=dk, dv=dv, chunk=chunk, n_chunks=n_chunks),
        grid=(s // rows,),
        in_specs=[pl.BlockSpec((rows, heads * dk), lambda i: (i, 0)),
                  pl.BlockSpec((rows, heads * dk), lambda i: (i, 0)),
                  pl.BlockSpec((rows, wv), lambda i: (i, v_blk)),
                  pl.BlockSpec((rows, wv), lambda i: (i, mo_blk)),
                  rep_spec, rep_spec, rep_spec,
                  pl.BlockSpec((n_chunks, heads, chunk), lambda i: (i, 0, 0)),
                  pl.BlockSpec((1, wv), lambda i: (0, 0))],
        out_specs=pl.BlockSpec((rows, wv), lambda i: (i, 0)),
        out_shape=jax.ShapeDtypeStruct((s, wv), BF16),
        scratch_shapes=[pltpu.VMEM((heads, dk, dv + LANES), F32),
                        pltpu.VMEM((heads, LANES), F32)],
        compiler_params=_params(1),
        name="mlstm",
    )(q, k, v, mo, b_rep, rmax_rep, a_rep, r_rows, g_out.reshape(1, wv).astype(F32))


def _sb_kernel(q_ref, k_ref, v_ref, o_ref, acc_ref, run_ref, live_ref, *, tk, hp, d):
    i = pl.program_id(1)
    row = lax.broadcasted_iota(jnp.int32, (tk, tk), 0)
    col = lax.broadcasted_iota(jnp.int32, (tk, tk), 1)
    diag_strict = col < row
    sums_rhs = jnp.concatenate([(row > col).astype(BF16), jnp.ones((tk, tk), BF16)], axis=1)
    sums_rhs = jnp.concatenate([sums_rhs, sums_rhs], axis=0)

    def walk(tiles):
        zs = [_dot_nt(t[0], t[1]) for t in tiles]
        log_betas, splits = [], []
        for z, (_, _, _, _, mask, _) in zip(zs, tiles):
            log_beta = jnp.minimum(z, 0.0) - jnp.log(1.0 + jnp.exp(-jnp.abs(z)))
            log_rest = log_beta - z
            if mask is not None:
                log_rest = jnp.where(mask, log_rest, 0.0)
            hi = log_rest.astype(BF16)
            lo = (log_rest - hi.astype(F32)).astype(BF16)
            log_betas.append(log_beta)
            splits.append(jnp.concatenate([hi, lo], axis=1))
        sums = [_dot(x, sums_rhs) for x in splits]
        weights, runs = [], []
        for log_beta, sm, (_, _, _, run, mask, after) in zip(log_betas, sums, tiles):
            if after is not None:
                run = runs[after]
            log_w = log_beta + sm[:, :tk]
            wts = jnp.exp(log_w if run is None else log_w + run)
            if mask is not None:
                wts = jnp.where(mask, wts, 0.0)
            weights.append(wts.astype(BF16))
            runs.append(sm[:, tk:] if run is None else run + sm[:, tk:])
        outs = [_dot(w, t[2]) for w, t in zip(weights, tiles)]
        return list(zip(outs, runs))

    def cols(h):
        return slice(h * d, (h + 1) * d)

    def chain(near, length, run_of):
        tiles = []
        for t in range(length):
            blk = near - t
            key_rows = pl.ds(pl.multiple_of(jnp.maximum(blk, 0) * tk, tk), tk)
            exists = jnp.broadcast_to(blk >= 0, (tk, tk))
            for h in range(hp):
                kk, vv = k_ref[key_rows, cols(h)], v_ref[key_rows, cols(h)]
                if t > 0:
                    tiles.append((tiles[h][0], kk, vv, None, exists, (t - 1) * hp + h))
                elif run_of is None:
                    tiles.append((q_ref[:, cols(h)], kk, vv, None, diag_strict, None))
                else:
                    tiles.append((q_ref[:, cols(h)], kk, vv, run_of(h), None, None))
        res = walk(tiles)
        return [(functools.reduce(jnp.add, [res[t * hp + h][0] for t in range(length)]),
                 res[(length - 1) * hp + h][1]) for h in range(hp)]

    for h, (out, run) in enumerate(chain(i, SB_STATIC_BLOCKS, None)):
        acc_ref[h] = out
        run_ref[h] = run
    live_ref[0] = (jnp.max(run_ref[...]) > SB_DONE_LOG).astype(jnp.int32)

    first = i - SB_STATIC_BLOCKS
    n_pairs = (jnp.maximum(first, -1) + 2) // 2

    def pair_body(step, carry):
        @pl.when(live_ref[0] > 0)
        def _():
            for h, (out, run) in enumerate(chain(first - 2 * step, 2, lambda h: run_ref[h])):
                acc_ref[h] += out
                run_ref[h] = run
            live_ref[0] = (jnp.max(run_ref[...]) > SB_DONE_LOG).astype(jnp.int32)
        return carry

    lax.fori_loop(0, n_pairs, pair_body, 0)
    for h in range(hp):
        o_ref[:, cols(h)] = acc_ref[h].astype(o_ref.dtype)


def stick_breaking(qkv, col0, heads, d):
    s = qkv.shape[0]
    tq = tk = SB_Q_ROWS
    hp = SB_HEADS_PER_STEP
    assert s % tq == 0 and heads % hp == 0 and col0 % (hp * d) == 0
    n_hg = heads // hp
    blk0 = col0 // (hp * d)
    once = pl.Buffered(1)
    return pl.pallas_call(
        functools.partial(_sb_kernel, tk=tk, hp=hp, d=d),
        grid=(n_hg, s // tq),
        in_specs=[pl.BlockSpec((tq, hp * d), lambda hg, i: (i, blk0 + hg)),
                  pl.BlockSpec((s, hp * d), lambda hg, i: (0, blk0 + n_hg + hg), pipeline_mode=once),
                  pl.BlockSpec((s, hp * d), lambda hg, i: (0, blk0 + 2 * n_hg + hg), pipeline_mode=once)],
        out_specs=pl.BlockSpec((tq, hp * d), lambda hg, i: (i, hg)),
        out_shape=jax.ShapeDtypeStruct((s, heads * d), BF16),
        scratch_shapes=[pltpu.VMEM((hp, tq, d), F32),
                        pltpu.VMEM((hp, tq, tk), F32),
                        pltpu.SMEM((1,), jnp.int32)],
        compiler_params=_params(2),
        name="stick_breaking",
    )(qkv, qkv, qkv)


def _ffn(x, xb, rstd, w1, w3, w2, next_gain):
    u, w2b = gateup(xb, rstd, w1, w3, w2)
    return matmul_residual(u, w2b, x, 0.5, tm_max=512, tn_max=512, next_gain=next_gain)


def _mixers(x, xb, rstd, w_in, conv_qk, b_igate, b_fgate, g_mlstm_out, w_proj_a, w_proj_b, w_out, next_gain):
    s, d_model = x.shape
    qk_w = ML_HEADS * ML_QK_DIM
    ml_w = ML_HEADS * ML_V_DIM
    sb_w = SB_HEADS * SB_HEAD_DIM
    o_mv = 2 * qk_w
    o_mo = o_mv + ml_w
    o_mi = o_mo + ml_w
    o_mf = o_mi + ML_HEADS
    o_sq = o_mf + ML_HEADS
    o_ga = o_sq + 3 * sb_w
    wt = jnp.swapaxes(w_in, 0, 1)
    zrows = jnp.zeros((LANES - ML_HEADS, d_model), F32)
    wt_gate = jnp.concatenate([wt[o_mi:o_mf], zrows, wt[o_mf:o_sq], zrows], axis=0)
    scale16 = jnp.concatenate([jnp.ones((ml_w,), F32),
                               jnp.full((sb_w,), SB_HEAD_DIM ** -0.5, F32),
                               jnp.ones((2 * sb_w,), F32)])

    p32 = project(xb, rstd, wt, [(0, o_mv), (o_mo, ml_w), (o_ga, 2 * d_model)], F32)
    p16 = project(xb, rstd, wt, [(o_mv, ml_w), (o_sq, 3 * sb_w)], BF16, col_scale=scale16)
    pg = project(xb, rstd, wt_gate, [(0, 2 * LANES)], F32)

    q, k = conv_silu_qk(p32, 2 * qk_w, conv_qk, ML_QK_DIM ** -0.5)
    r, b_rep, rmax_rep, a_rep = gate_prep(pg, b_igate, b_fgate, ML_CHUNK, ML_HEADS)
    r_rows = jnp.swapaxes(r[:, :ML_HEADS].reshape(s // ML_CHUNK, ML_CHUNK, ML_HEADS), 1, 2)
    ya = mlstm(q, k, p16, 0, p32, 2 * qk_w, b_rep, rmax_rep, a_rep, r_rows, g_mlstm_out,
               ML_HEADS, ML_QK_DIM, ML_V_DIM, ML_CHUNK)

    yb = stick_breaking(p16, ml_w, SB_HEADS, SB_HEAD_DIM)

    merged = gated_merge(ya, yb, w_proj_a, w_proj_b, p32, 2 * qk_w + ml_w, 2 * qk_w + ml_w + d_model)
    return matmul_residual(merged, w_out, x, 1.0, tm_max=1024, tn_max=512, next_gain=next_gain)


def kernel(x, g_ffn1, w1_ffn1, w3_ffn1, w2_ffn1, g_mix, w_in, conv_qk, b_igate, b_fgate,
           g_mlstm_out, w_proj_a, w_proj_b, w_out, g_ffn2, w1_ffn2, w3_ffn2, w2_ffn2, g_final):
    batch, seq, d_model = x.shape
    outs = []
    depth = g_ffn1.shape[0]
    for bi in range(batch):
        xf = x[bi]
        xb, rstd = cast_stats(xf, g_ffn1[0])
        for l in range(depth):
            xf, xb, rstd = _ffn(xf, xb, rstd, w1_ffn1[l], w3_ffn1[l], w2_ffn1[l], g_mix[l])
            xf, xb, rstd = _mixers(xf, xb, rstd, w_in[l], conv_qk[l], b_igate[l], b_fgate[l],
                                   g_mlstm_out[l], w_proj_a[l], w_proj_b[l], w_out[l], g_ffn2[l])
            if l + 1 < depth:
                xf, xb, rstd = _ffn(xf, xb, rstd, w1_ffn2[l], w3_ffn2[l], w2_ffn2[l], g_ffn1[l + 1])
            else:
                xf = _ffn(xf, xb, rstd, w1_ffn2[l], w3_ffn2[l], w2_ffn2[l], None)
        outs.append(rmsnorm(xf, g_final, x.dtype))
    return jnp.stack(outs, axis=0)
```

```python
import functools

import jax
import jax.numpy as jnp
from jax import lax
from jax.experimental import pallas as pl
from jax.experimental.pallas import tpu as pltpu

F32 = jnp.float32
BF16 = jnp.bfloat16
EPS = 1e-6

V7X_VMEM_BYTES = 64 * 1024 * 1024
VMEM_LIMIT_BYTES = V7X_VMEM_BYTES - 8 * 1024 * 1024
LANES = 128
BF16_SUBLANES = 16

ML_HEADS = 8
ML_QK_DIM = 128
ML_V_DIM = 256
ML_CHUNK = 64
SB_HEADS = 16
SB_HEAD_DIM = 128
EPILOGUE_ROWS = 256
SB_HEADS_PER_STEP = 8
SB_Q_ROWS = 128
SB_STATIC_BLOCKS = 3
CONV_WIDTH = 4
SB_DONE_LOG = -88.0


def _params(n_grid):
    return pltpu.CompilerParams(
        dimension_semantics=("arbitrary",) * n_grid,
        vmem_limit_bytes=VMEM_LIMIT_BYTES)


def _pick(n, candidates):
    for c in candidates:
        if n % c == 0:
            return c
    raise ValueError(f"no tile in {candidates} divides {n}")


def _log_sigmoid(x):
    return jnp.minimum(x, 0.0) - jnp.log1p(jnp.exp(-jnp.abs(x)))


def _dot(a, b):
    return jnp.dot(a, b, preferred_element_type=F32)


def _dot_nt(a, b):
    return lax.dot_general(a, b, (((1,), (1,)), ((), ())), preferred_element_type=F32)


def _rep(x, n):
    return x if n == 1 else jnp.concatenate([x] * n, axis=1)


def _rmsnorm_kernel(x_ref, g_ref, o_ref):
    x = x_ref[...]
    ms = jnp.mean(x * x, axis=-1, keepdims=True)
    o_ref[...] = (x * lax.rsqrt(ms + EPS) * g_ref[...]).astype(o_ref.dtype)


def rmsnorm(x, g, out_dtype):
    s, d = x.shape
    tr = _pick(s, (256, 128, 64, 8))
    return pl.pallas_call(
        _rmsnorm_kernel,
        grid=(s // tr,),
        in_specs=[pl.BlockSpec((tr, d), lambda i: (i, 0)),
                  pl.BlockSpec((1, d), lambda i: (0, 0))],
        out_specs=pl.BlockSpec((tr, d), lambda i: (i, 0)),
        out_shape=jax.ShapeDtypeStruct((s, d), out_dtype),
        compiler_params=_params(1),
        name="rmsnorm",
    )(x, g.reshape(1, d).astype(F32))


def _cast_stats_kernel(x_ref, g_ref, xb_ref, rstd_ref):
    x = x_ref[...]
    xb_ref[...] = (x * g_ref[...]).astype(BF16)
    ms = jnp.mean(x * x, axis=-1, keepdims=True)
    rstd_ref[...] = jnp.broadcast_to(lax.rsqrt(ms + EPS), rstd_ref.shape)


def cast_stats(x, gain):
    s, d = x.shape
    tr = _pick(s, (256, 128, 64, 8))
    return pl.pallas_call(
        _cast_stats_kernel,
        grid=(s // tr,),
        in_specs=[pl.BlockSpec((tr, d), lambda i: (i, 0)),
                  pl.BlockSpec((1, d), lambda i: (0, 0))],
        out_specs=[pl.BlockSpec((tr, d), lambda i: (i, 0)),
                   pl.BlockSpec((tr, LANES), lambda i: (i, 0))],
        out_shape=[jax.ShapeDtypeStruct((s, d), BF16),
                   jax.ShapeDtypeStruct((s, LANES), F32)],
        compiler_params=_params(1),
        name="cast_stats",
    )(x, gain.reshape(1, d).astype(F32))


def _row_chunks(n_rows):
    rows = min(EPILOGUE_ROWS, n_rows)
    assert n_rows % rows == 0
    return [slice(r, r + rows) for r in range(0, n_rows, rows)]


def _gateup_kernel(a_ref, rstd_ref, w1_ref, w3_ref, w2_ref, o_ref, w2b_ref):
    w1 = w1_ref[...].astype(BF16)
    w3 = w3_ref[...].astype(BF16)
    for sl in _row_chunks(o_ref.shape[0]):
        a = a_ref[sl, :]
        rstd = _rep(rstd_ref[sl, :], o_ref.shape[1] // LANES)
        g = rstd * _dot(a, w1)
        u = rstd * _dot(a, w3)
        o_ref[sl, :] = (g * jax.nn.sigmoid(g) * u).astype(o_ref.dtype)
    w2b_ref[...] = w2_ref[...].astype(BF16)


def gateup(xb, rstd, w1, w3, w2):
    m, k = xb.shape
    n = w1.shape[1]
    k2, n2 = w2.shape
    tm = _pick(m, (2048, 1024, 512, 256, 128, 64, 8))
    tn = _pick(n, (256, 128))
    n_i, n_j = m // tm, n // tn
    slab = k2 // (n_i * n_j)
    assert slab * n_i * n_j == k2 and slab % BF16_SUBLANES == 0, (k2, n_i, n_j)
    return pl.pallas_call(
        _gateup_kernel,
        grid=(n_i, n_j),
        in_specs=[pl.BlockSpec((tm, k), lambda i, j: (i, 0), pipeline_mode=pl.Buffered(1)),
                  pl.BlockSpec((tm, LANES), lambda i, j: (i, 0)),
                  pl.BlockSpec((k, tn), lambda i, j: (0, j)),
                  pl.BlockSpec((k, tn), lambda i, j: (0, j)),
                  pl.BlockSpec((slab, n2), lambda i, j: (i * n_j + j, 0))],
        out_specs=[pl.BlockSpec((tm, tn), lambda i, j: (i, j)),
                   pl.BlockSpec((slab, n2), lambda i, j: (i * n_j + j, 0))],
        out_shape=[jax.ShapeDtypeStruct((m, n), BF16),
                   jax.ShapeDtypeStruct((k2, n2), BF16)],
        compiler_params=_params(2),
        name="ffn_gateup",
    )(xb, rstd, w1, w3, w2)


def _resid_kernel(a_ref, w_ref, x_ref, o_ref, *, alpha):
    w = w_ref[...].astype(BF16)
    for sl in _row_chunks(o_ref.shape[0]):
        o_ref[sl, :] = x_ref[sl, :] + alpha * _dot(a_ref[sl, :], w)


def _resid_stats_kernel(a_ref, w_ref, x_ref, g_ref, o_ref, ob_ref, rstd_ref, ssq_ref, *, alpha, n_total):
    j = pl.program_id(1)
    w = w_ref[...].astype(BF16)
    parts = []
    for sl in _row_chunks(o_ref.shape[0]):
        y = x_ref[sl, :] + alpha * _dot(a_ref[sl, :], w)
        o_ref[sl, :] = y
        ob_ref[sl, :] = (y * g_ref[...]).astype(BF16)
        sq = y * y
        part = sq[:, :LANES]
        for t in range(1, sq.shape[1] // LANES):
            part = part + sq[:, t * LANES:(t + 1) * LANES]
        parts.append(part)
    part = jnp.concatenate(parts, axis=0)

    @pl.when(j == 0)
    def _():
        ssq_ref[...] = part

    @pl.when(j > 0)
    def _():
        ssq_ref[...] += part

    @pl.when(j == pl.num_programs(1) - 1)
    def _():
        ms = jnp.sum(ssq_ref[...], axis=-1, keepdims=True) * (1.0 / n_total)
        rstd_ref[...] = jnp.broadcast_to(lax.rsqrt(ms + EPS), rstd_ref.shape)


def matmul_residual(a, w, x, alpha, tm_max, tn_max, next_gain=None):
    m, k = a.shape
    n = w.shape[1]
    tm = _pick(m, tuple(t for t in (1024, 512, 256, 128, 64, 8) if t <= tm_max))
    tn = _pick(n, tuple(t for t in (512, 256, 128) if t <= tn_max))
    tile = pl.BlockSpec((tm, tn), lambda i, j: (i, j))
    in_specs = [pl.BlockSpec((tm, k), lambda i, j: (i, 0)),
                pl.BlockSpec((k, tn), lambda i, j: (0, j)),
                tile]
    if next_gain is None:
        return pl.pallas_call(
            functools.partial(_resid_kernel, alpha=alpha),
            grid=(m // tm, n // tn),
            in_specs=in_specs,
            out_specs=tile,
            out_shape=jax.ShapeDtypeStruct((m, n), F32),
            compiler_params=_params(2),
            name="matmul_residual",
        )(a, w, x)
    return pl.pallas_call(
        functools.partial(_resid_stats_kernel, alpha=alpha, n_total=n),
        grid=(m // tm, n // tn),
        in_specs=in_specs + [pl.BlockSpec((1, tn), lambda i, j: (0, j))],
        out_specs=[tile, tile, pl.BlockSpec((tm, LANES), lambda i, j: (i, 0))],
        out_shape=[jax.ShapeDtypeStruct((m, n), F32),
                   jax.ShapeDtypeStruct((m, n), BF16),
                   jax.ShapeDtypeStruct((m, LANES), F32)],
        scratch_shapes=[pltpu.VMEM((tm, LANES), F32)],
        compiler_params=_params(2),
        name="matmul_residual_stats",
    )(a, w, x, next_gain.reshape(1, n).astype(F32))


def _proj_kernel(a_ref, rstd_ref, wt_ref, o_ref):
    wt = wt_ref[...].astype(BF16)
    for sl in _row_chunks(o_ref.shape[0]):
        y = _dot_nt(a_ref[sl, :], wt)
        o_ref[sl, :] = (_rep(rstd_ref[sl, :], y.shape[1] // LANES) * y).astype(o_ref.dtype)


def _proj_scaled_kernel(a_ref, rstd_ref, wt_ref, s_ref, o_ref):
    wt = wt_ref[...].astype(BF16)
    for sl in _row_chunks(o_ref.shape[0]):
        y = _dot_nt(a_ref[sl, :], wt)
        o_ref[sl, :] = (_rep(rstd_ref[sl, :], y.shape[1] // LANES) * y * s_ref[...]).astype(o_ref.dtype)


def project(xb, rstd, wt, segments, out_dtype, col_scale=None):
    m, k = xb.shape
    n = sum(seg_n for _, seg_n in segments)
    tn = 512 if all(seg_n % 512 == 0 for _, seg_n in segments) else 256
    assert all(seg_n % tn == 0 and row0 % 8 == 0 for row0, seg_n in segments)
    if n == tn:
        tm = _pick(m, (512, 256, 128, 64, 8))
        panel = pl.BlockSpec((tm, k), lambda i, j: (i, 0))
    else:
        tm = _pick(m, (2048, 1024, 512, 256, 128, 64, 8))
        panel = pl.BlockSpec((tm, k), lambda i, j: (i, 0), pipeline_mode=pl.Buffered(1))

    def wt_row(j):
        blk0 = 0
        row8 = None
        for row0, seg_n in segments:
            here = row0 // 8 + (j - blk0) * (tn // 8)
            row8 = here if row8 is None else jnp.where(j >= blk0, here, row8)
            blk0 += seg_n // tn
        return row8 * 8

    in_specs = [panel,
                pl.BlockSpec((tm, LANES), lambda i, j: (i, 0)),
                pl.BlockSpec((pl.Element(tn), pl.Element(k)), lambda i, j: (wt_row(j), 0))]
    args = [xb, rstd, wt]
    body = _proj_kernel
    if col_scale is not None:
        in_specs.append(pl.BlockSpec((1, tn), lambda i, j: (0, j)))
        args.append(col_scale.reshape(1, n).astype(F32))
        body = _proj_scaled_kernel
    return pl.pallas_call(
        body,
        grid=(m // tm, n // tn),
        in_specs=in_specs,
        out_specs=pl.BlockSpec((tm, tn), lambda i, j: (i, j)),
        out_shape=jax.ShapeDtypeStruct((m, n), out_dtype),
        compiler_params=_params(2),
        name="in_project",
    )(*args)


def _merge_kernel(ya_ref, yb_ref, wa_ref, wb_ref, ga_ref, gb_ref, o_ref):
    wa = wa_ref[...].astype(BF16)
    wb = wb_ref[...].astype(BF16)
    for sl in _row_chunks(o_ref.shape[0]):
        pa = _dot(ya_ref[sl, :], wa)
        pb = _dot(yb_ref[sl, :], wb)
        o_ref[sl, :] = (jax.nn.sigmoid(ga_ref[sl, :]) * pa
                        + jax.nn.sigmoid(gb_ref[sl, :]) * pb).astype(o_ref.dtype)


def gated_merge(ya, yb, wa, wb, gates, ga_col0, gb_col0):
    m, k = ya.shape
    n = wa.shape[1]
    tm = _pick(m, (1024, 512, 256, 128, 64, 8))
    tn = _pick(n, (512, 256, 128))
    assert ga_col0 % tn == 0 and gb_col0 % tn == 0
    ga_blk, gb_blk = ga_col0 // tn, gb_col0 // tn
    return pl.pallas_call(
        _merge_kernel,
        grid=(m // tm, n // tn),
        in_specs=[pl.BlockSpec((tm, k), lambda i, j: (i, 0)),
                  pl.BlockSpec((tm, k), lambda i, j: (i, 0)),
                  pl.BlockSpec((k, tn), lambda i, j: (0, j)),
                  pl.BlockSpec((k, tn), lambda i, j: (0, j)),
                  pl.BlockSpec((tm, tn), lambda i, j: (i, ga_blk + j)),
                  pl.BlockSpec((tm, tn), lambda i, j: (i, gb_blk + j))],
        out_specs=pl.BlockSpec((tm, tn), lambda i, j: (i, j)),
        out_shape=jax.ShapeDtypeStruct((m, n), BF16),
        compiler_params=_params(2),
        name="gated_merge",
    )(ya, yb, wa, wb, gates, gates)


def _conv_kernel(cur_ref, prev_ref, w_ref, q_ref, k_ref, *, k_scale):
    cur = cur_ref[...]
    prev = jnp.where(pl.program_id(0) > 0, prev_ref[...], 0.0)
    ext = jnp.concatenate([prev, cur], axis=0)
    w = w_ref[...]
    tr = cur.shape[0]
    acc = w[CONV_WIDTH - 1:CONV_WIDTH] * cur
    for j in range(CONV_WIDTH - 1):
        off = 8 - (CONV_WIDTH - 1) + j
        acc = acc + w[j:j + 1] * ext[off:off + tr]
    y = acc * jax.nn.sigmoid(acc)
    half = y.shape[1] // 2
    q_ref[...] = y[:, :half].astype(q_ref.dtype)
    k_ref[...] = (y[:, half:] * k_scale).astype(k_ref.dtype)


def conv_silu_qk(p, width2, conv_w, k_scale):
    s = p.shape[0]
    tr = _pick(s, (256, 128, 64, 8))
    half = width2 // 2
    return pl.pallas_call(
        functools.partial(_conv_kernel, k_scale=k_scale),
        grid=(s // tr,),
        in_specs=[pl.BlockSpec((tr, width2), lambda i: (i, 0)),
                  pl.BlockSpec((8, width2), lambda i: (jnp.maximum(i * (tr // 8) - 1, 0), 0)),
                  pl.BlockSpec((CONV_WIDTH, width2), lambda i: (0, 0))],
        out_specs=[pl.BlockSpec((tr, half), lambda i: (i, 0)),
                   pl.BlockSpec((tr, half), lambda i: (i, 0))],
        out_shape=[jax.ShapeDtypeStruct((s, half), BF16),
                   jax.ShapeDtypeStruct((s, half), BF16)],
        compiler_params=_params(1),
        name="conv_silu_qk",
    )(p, p, conv_w.astype(F32))


def _split3(x):
    hi = x.astype(BF16)
    rem = x - hi.astype(F32)
    mid = rem.astype(BF16)
    lo = (rem - mid.astype(F32)).astype(BF16)
    return jnp.concatenate([hi, mid, lo], axis=1)


def _gate_kernel(g_ref, bi_ref, bf_ref, e_ref, r_ref, brep_ref, rmaxrep_ref, arep_ref, *, chunk):
    g = g_ref[...]
    rows = g.shape[0]
    li = g[:, :LANES] + bi_ref[...]
    lf = _log_sigmoid(g[:, LANES:] + bf_ref[...])
    pos = lax.broadcasted_iota(jnp.int32, lf.shape, 0) % chunk

    def scan(x, op, fill, down):
        d = 1
        while d < chunk:
            if down:
                x = op(x, jnp.where(pos >= d, pltpu.roll(x, d, 0), fill))
            else:
                x = op(x, jnp.where(pos + d < chunk, pltpu.roll(x, rows - d, 0), fill))
            d *= 2
        return x

    b = scan(lf, jnp.add, 0.0, True)
    r = li - b
    rmax = scan(r, jnp.maximum, -jnp.inf, True)
    a = (scan(lf, jnp.add, 0.0, False) - lf) + li
    r_ref[...] = r
    for x, ref in ((b, brep_ref), (rmax, rmaxrep_ref), (a, arep_ref)):
        ref[...] = _dot(_split3(x), e_ref[...])


def gate_prep(g, b_i, b_f, chunk, heads):
    s = g.shape[0]
    tr = _pick(s, (512, 256, 128, 64))
    pad = lambda v: jnp.zeros((1, LANES), F32).at[0, :v.shape[0]].set(v.astype(F32))
    src_lane = lax.broadcasted_iota(jnp.int32, (3 * LANES, heads * LANES), 0) % LANES
    dst_head = lax.broadcasted_iota(jnp.int32, (3 * LANES, heads * LANES), 1) // LANES
    expand = (src_lane == dst_head).astype(BF16)
    rep = jax.ShapeDtypeStruct((s, heads * LANES), F32)
    return pl.pallas_call(
        functools.partial(_gate_kernel, chunk=chunk),
        grid=(s // tr,),
        in_specs=[pl.BlockSpec((tr, 2 * LANES), lambda i: (i, 0)),
                  pl.BlockSpec((1, LANES), lambda i: (0, 0)),
                  pl.BlockSpec((1, LANES), lambda i: (0, 0)),
                  pl.BlockSpec((3 * LANES, heads * LANES), lambda i: (0, 0))],
        out_specs=[pl.BlockSpec((tr, LANES), lambda i: (i, 0))]
                  + [pl.BlockSpec((tr, heads * LANES), lambda i: (i, 0))] * 3,
        out_shape=[jax.ShapeDtypeStruct((s, LANES), F32), rep, rep, rep],
        compiler_params=_params(1),
        name="gate_prep",
    )(g, pad(b_i), pad(b_f), expand)


def _mlstm_kernel(q_ref, k_ref, v_ref, mo_ref, brep_ref, rmaxrep_ref, arep_ref, r_ref, g_ref, o_ref,
                  cn_ref, m_ref, *, heads, dk, dv, chunk, n_chunks):
    @pl.when(pl.program_id(0) == 0)
    def _():
        cn_ref[...] = jnp.zeros_like(cn_ref)
        m_ref[...] = jnp.zeros_like(m_ref)

    row = lax.broadcasted_iota(jnp.int32, (chunk, chunk), 0)
    col = lax.broadcasted_iota(jnp.int32, (chunk, chunk), 1)
    tril = row >= col
    ones_ext = jnp.ones((chunk, LANES), BF16)
    ones_sum = jnp.ones((2 * dv, LANES), BF16)
    v_tiles = dv // LANES
    rep = _rep

    def lanes(h):
        return slice(h * LANES, (h + 1) * LANES)

    def chunk_body(c, carry):
        r0 = pl.multiple_of(c * chunk, chunk)
        rows = pl.ds(r0, chunk)
        tail = pl.ds(pl.multiple_of(r0 + chunk - 8, 8), 8)
        r_all = r_ref[c]
        hs = range(heads)
        q = [q_ref[rows, h * dk:(h + 1) * dk] for h in hs]
        k = [k_ref[rows, h * dk:(h + 1) * dk] for h in hs]
        v_ext = [jnp.concatenate([v_ref[rows, h * dv:(h + 1) * dv], ones_ext], axis=1) for h in hs]
        cn = [cn_ref[h] for h in hs]
        s = [_dot_nt(q[h], k[h]) for h in hs]
        inter = [_dot(q[h], cn[h].astype(BF16)) for h in hs]

        m_prev = [m_ref[h:h + 1, :] for h in hs]
        big_m = [jnp.maximum(m_prev[h], rmaxrep_ref[rows, lanes(h)]) for h in hs]
        w = []
        for h in hs:
            d_exp = jnp.exp(r_all[h:h + 1, :] - big_m[h][:, :chunk])
            w.append((s[h] * jnp.where(tril, d_exp, 0.0)).astype(BF16))
        intra = [_dot(w[h], v_ext[h]) for h in hs]

        for h in hs:
            decay = jnp.exp(m_prev[h] - big_m[h])
            tot = rep(decay, v_tiles + 1) * inter[h] + intra[h]
            floor = jnp.exp(-(brep_ref[rows, lanes(h)] + big_m[h]))
            denom = jnp.maximum(jnp.abs(tot[:, dv:]), floor)
            hh = tot[:, :dv] / rep(denom, v_tiles)
            sq = hh * hh
            sq_hi = sq.astype(BF16)
            sq_lo = (sq - sq_hi.astype(F32)).astype(BF16)
            ms = _dot(jnp.concatenate([sq_hi, sq_lo], axis=1), ones_sum) * (1.0 / dv)
            y = hh * rep(lax.rsqrt(ms + EPS), v_tiles) * g_ref[:, h * dv:(h + 1) * dv]
            y = y * jax.nn.sigmoid(mo_ref[rows, h * dv:(h + 1) * dv])
            o_ref[rows, h * dv:(h + 1) * dv] = y.astype(o_ref.dtype)

        sv, m_new, carry_scale = [], [], []
        for h in hs:
            a_rep = arep_ref[rows, lanes(h)]
            b_last = brep_ref[tail, lanes(h)][7:8, :]
            m_new.append(jnp.maximum(b_last + m_prev[h], jnp.max(a_rep, axis=0, keepdims=True)))
            carry_scale.append(jnp.exp(b_last + m_prev[h] - m_new[h]))
            src = jnp.exp(a_rep - m_new[h])
            sv.append((rep(src, v_tiles + 1) * v_ext[h].astype(F32)).astype(BF16))
        upd = [lax.dot_general(k[h], sv[h], (((0,), (0,)), ((), ())), preferred_element_type=F32)
               for h in hs]
        for h in hs:
            cn_ref[h] = rep(carry_scale[h], v_tiles + 1) * cn[h] + upd[h]
            m_ref[h:h + 1, :] = m_new[h]
        return carry

    lax.fori_loop(0, n_chunks, chunk_body, 0)


def mlstm(q, k, v, v_col0, mo, mo_col0, b_rep, rmax_rep, a_rep, r_rows, g_out, heads, dk, dv, chunk):
    s = q.shape[0]
    rows = _pick(s, (512, 256, 128, 64))
    n_chunks = rows // chunk
    wv = heads * dv
    assert v_col0 % wv == 0 and mo_col0 % wv == 0 and dv % LANES == 0
    v_blk, mo_blk = v_col0 // wv, mo_col0 // wv
    rep_spec = pl.BlockSpec((rows, heads * LANES), lambda i: (i, 0))
    return pl.pallas_call(
        functools.partial(_mlstm_kernel, heads=heads, dk=dk, dv=dv, chunk=chunk, n_chunks=n_chunks),
        grid=(s // rows,),
        in_specs=[pl.BlockSpec((rows, heads * dk), lambda i: (i, 0)),
                  pl.BlockSpec((rows, heads * dk), lambda i: (i, 0)),
                  pl.BlockSpec((rows, wv), lambda i: (i, v_blk)),
                  pl.BlockSpec((rows, wv), lambda i: (i, mo_blk)),
                  rep_spec, rep_spec, rep_spec,
                  pl.BlockSpec((n_chunks, heads, chunk), lambda i: (i, 0, 0)),
                  pl.BlockSpec((1, wv), lambda i: (0, 0))],
        out_specs=pl.BlockSpec((rows, wv), lambda i: (i, 0)),
        out_shape=jax.ShapeDtypeStruct((s, wv), BF16),
        scratch_shapes=[pltpu.VMEM((heads, dk, dv + LANES), F32),
                        pltpu.VMEM((heads, LANES), F32)],
        compiler_params=_params(1),
        name="mlstm",
    )(q, k, v, mo, b_rep, rmax_rep, a_rep, r_rows, g_out.reshape(1, wv).astype(F32))


def _sb_kernel(q_ref, k_ref, v_ref, o_ref, acc_ref, run_ref, live_ref, *, tk, hp, d):
    i = pl.program_id(1)
    row = lax.broadcasted_iota(jnp.int32, (tk, tk), 0)
    col = lax.broadcasted_iota(jnp.int32, (tk, tk), 1)
    diag_strict = col < row
    sums_rhs = jnp.concatenate([(row > col).astype(BF16), jnp.ones((tk, tk), BF16)], axis=1)
    sums_rhs = jnp.concatenate([sums_rhs, sums_rhs], axis=0)

    def walk(tiles):
        zs = [_dot_nt(t[0], t[1]) for t in tiles]
        log_betas, splits = [], []
        for z, (_, _, _, _, mask, _) in zip(zs, tiles):
            log_beta = jnp.minimum(z, 0.0) - jnp.log(1.0 + jnp.exp(-jnp.abs(z)))
            log_rest = log_beta - z
            if mask is not None:
                log_rest = jnp.where(mask, log_rest, 0.0)
            hi = log_rest.astype(BF16)
            lo = (log_rest - hi.astype(F32)).astype(BF16)
            log_betas.append(log_beta)
            splits.append(jnp.concatenate([hi, lo], axis=1))
        sums = [_dot(x, sums_rhs) for x in splits]
        weights, runs = [], []
        for log_beta, sm, (_, _, _, run, mask, after) in zip(log_betas, sums, tiles):
            if after is not None:
                run = runs[after]
            log_w = log_beta + sm[:, :tk]
            wts = jnp.exp(log_w if run is None else log_w + run)
            if mask is not None:
                wts = jnp.where(mask, wts, 0.0)
            weights.append(wts.astype(BF16))
            runs.append(sm[:, tk:] if run is None else run + sm[:, tk:])
        outs = [_dot(w, t[2]) for w, t in zip(weights, tiles)]
        return list(zip(outs, runs))

    def cols(h):
        return slice(h * d, (h + 1) * d)

    def chain(near, length, run_of):
        tiles = []
        for t in range(length):
            blk = near - t
            key_rows = pl.ds(pl.multiple_of(jnp.maximum(blk, 0) * tk, tk), tk)
            exists = jnp.broadcast_to(blk >= 0, (tk, tk))
            for h in range(hp):
                kk, vv = k_ref[key_rows, cols(h)], v_ref[key_rows, cols(h)]
                if t > 0:
                    tiles.append((tiles[h][0], kk, vv, None, exists, (t - 1) * hp + h))
                elif run_of is None:
                    tiles.append((q_ref[:, cols(h)], kk, vv, None, diag_strict, None))
                else:
                    tiles.append((q_ref[:, cols(h)], kk, vv, run_of(h), None, None))
        res = walk(tiles)
        return [(functools.reduce(jnp.add, [res[t * hp + h][0] for t in range(length)]),
                 res[(length - 1) * hp + h][1]) for h in range(hp)]

    for h, (out, run) in enumerate(chain(i, SB_STATIC_BLOCKS, None)):
        acc_ref[h] = out
        run_ref[h] = run
    live_ref[0] = (jnp.max(run_ref[...]) > SB_DONE_LOG).astype(jnp.int32)

    first = i - SB_STATIC_BLOCKS

    def block_body(step, carry):
        @pl.when(live_ref[0] > 0)
        def _():
            for h, (out, run) in enumerate(chain(first - step, 1, lambda h: run_ref[h])):
                acc_ref[h] += out
                run_ref[h] = run
            live_ref[0] = (jnp.max(run_ref[...]) > SB_DONE_LOG).astype(jnp.int32)
        return carry

    lax.fori_loop(0, jnp.maximum(first + 1, 0), block_body, 0)
    for h in range(hp):
        o_ref[:, cols(h)] = acc_ref[h].astype(o_ref.dtype)


def stick_breaking(qkv, col0, heads, d):
    s = qkv.shape[0]
    tq = tk = SB_Q_ROWS
    hp = SB_HEADS_PER_STEP
    assert s % tq == 0 and heads % hp == 0 and col0 % (hp * d) == 0
    n_hg = heads // hp
    blk0 = col0 // (hp * d)
    once = pl.Buffered(1)
    return pl.pallas_call(
        functools.partial(_sb_kernel, tk=tk, hp=hp, d=d),
        grid=(n_hg, s // tq),
        in_specs=[pl.BlockSpec((tq, hp * d), lambda hg, i: (i, blk0 + hg)),
                  pl.BlockSpec((s, hp * d), lambda hg, i: (0, blk0 + n_hg + hg), pipeline_mode=once),
                  pl.BlockSpec((s, hp * d), lambda hg, i: (0, blk0 + 2 * n_hg + hg), pipeline_mode=once)],
        out_specs=pl.BlockSpec((tq, hp * d), lambda hg, i: (i, hg)),
        out_shape=jax.ShapeDtypeStruct((s, heads * d), BF16),
        scratch_shapes=[pltpu.VMEM((hp, tq, d), F32),
                        pltpu.VMEM((hp, tq, tk), F32),
                        pltpu.SMEM((1,), jnp.int32)],
        compiler_params=_params(2),
        name="stick_breaking",
    )(qkv, qkv, qkv)


def _ffn(x, xb, rstd, w1, w3, w2, next_gain):
    u, w2b = gateup(xb, rstd, w1, w3, w2)
    return matmul_residual(u, w2b, x, 0.5, tm_max=512, tn_max=512, next_gain=next_gain)


def _mixers(x, xb, rstd, w_in, conv_qk, b_igate, b_fgate, g_mlstm_out, w_proj_a, w_proj_b, w_out, next_gain):
    s, d_model = x.shape
    qk_w = ML_HEADS * ML_QK_DIM
    ml_w = ML_HEADS * ML_V_DIM
    sb_w = SB_HEADS * SB_HEAD_DIM
    o_mv = 2 * qk_w
    o_mo = o_mv + ml_w
    o_mi = o_mo + ml_w
    o_mf = o_mi + ML_HEADS
    o_sq = o_mf + ML_HEADS
    o_ga = o_sq + 3 * sb_w
    wt = jnp.swapaxes(w_in, 0, 1)
    zrows = jnp.zeros((LANES - ML_HEADS, d_model), F32)
    wt_gate = jnp.concatenate([wt[o_mi:o_mf], zrows, wt[o_mf:o_sq], zrows], axis=0)
    scale16 = jnp.concatenate([jnp.ones((ml_w,), F32),
                               jnp.full((sb_w,), SB_HEAD_DIM ** -0.5, F32),
                               jnp.ones((2 * sb_w,), F32)])

    p32 = project(xb, rstd, wt, [(0, o_mv), (o_mo, ml_w), (o_ga, 2 * d_model)], F32)
    p16 = project(xb, rstd, wt, [(o_mv, ml_w), (o_sq, 3 * sb_w)], BF16, col_scale=scale16)
    pg = project(xb, rstd, wt_gate, [(0, 2 * LANES)], F32)

    q, k = conv_silu_qk(p32, 2 * qk_w, conv_qk, ML_QK_DIM ** -0.5)
    r, b_rep, rmax_rep, a_rep = gate_prep(pg, b_igate, b_fgate, ML_CHUNK, ML_HEADS)
    r_rows = jnp.swapaxes(r[:, :ML_HEADS].reshape(s // ML_CHUNK, ML_CHUNK, ML_HEADS), 1, 2)
    ya = mlstm(q, k, p16, 0, p32, 2 * qk_w, b_rep, rmax_rep, a_rep, r_rows, g_mlstm_out,
               ML_HEADS, ML_QK_DIM, ML_V_DIM, ML_CHUNK)

    yb = stick_breaking(p16, ml_w, SB_HEADS, SB_HEAD_DIM)

    merged = gated_merge(ya, yb, w_proj_a, w_proj_b, p32, 2 * qk_w + ml_w, 2 * qk_w + ml_w + d_model)
    return matmul_residual(merged, w_out, x, 1.0, tm_max=1024, tn_max=512, next_gain=next_gain)


def kernel(x, g_ffn1, w1_ffn1, w3_ffn1, w2_ffn1, g_mix, w_in, conv_qk, b_igate, b_fgate,
           g_mlstm_out, w_proj_a, w_proj_b, w_out, g_ffn2, w1_ffn2, w3_ffn2, w2_ffn2, g_final):
    batch, seq, d_model = x.shape
    outs = []
    depth = g_ffn1.shape[0]
    for bi in range(batch):
        xf = x[bi]
        xb, rstd = cast_stats(xf, g_ffn1[0])
        for l in range(depth):
            xf, xb, rstd = _ffn(xf, xb, rstd, w1_ffn1[l], w3_ffn1[l], w2_ffn1[l], g_mix[l])
            xf, xb, rstd = _mixers(xf, xb, rstd, w_in[l], conv_qk[l], b_igate[l], b_fgate[l],
                                   g_mlstm_out[l], w_proj_a[l], w_proj_b[l], w_out[l], g_ffn2[l])
            if l + 1 < depth:
                xf, xb, rstd = _ffn(xf, xb, rstd, w1_ffn2[l], w3_ffn2[l], w2_ffn2[l], g_ffn1[l + 1])
            else:
                xf = _ffn(xf, xb, rstd, w1_ffn2[l], w3_ffn2[l], w2_ffn2[l], None)
        outs.append(rmsnorm(xf, g_final, x.dtype))
    return jnp.stack(outs, axis=0)
```

```python
import functools

import jax
import jax.numpy as jnp
from jax import lax
from jax.experimental import pallas as pl
from jax.experimental.pallas import tpu as pltpu

F32 = jnp.float32
BF16 = jnp.bfloat16
EPS = 1e-6

V7X_VMEM_BYTES = 64 * 1024 * 1024
VMEM_LIMIT_BYTES = V7X_VMEM_BYTES - 8 * 1024 * 1024
LANES = 128
BF16_SUBLANES = 16

ML_HEADS = 8
ML_QK_DIM = 128
ML_V_DIM = 256
ML_CHUNK = 64
SB_HEADS = 16
SB_HEAD_DIM = 128
EPILOGUE_ROWS = 256
SB_HEADS_PER_STEP = 8
SB_Q_ROWS = 128
SB_STATIC_BLOCKS = 3
CONV_WIDTH = 4
SB_DONE_LOG = -88.0


def _params(n_grid):
    return pltpu.CompilerParams(
        dimension_semantics=("arbitrary",) * n_grid,
        vmem_limit_bytes=VMEM_LIMIT_BYTES)


def _pick(n, candidates):
    for c in candidates:
        if n % c == 0:
            return c
    raise ValueError(f"no tile in {candidates} divides {n}")


def _log_sigmoid(x):
    return jnp.minimum(x, 0.0) - jnp.log1p(jnp.exp(-jnp.abs(x)))


def _dot(a, b):
    return jnp.dot(a, b, preferred_element_type=F32)


def _dot_nt(a, b):
    return lax.dot_general(a, b, (((1,), (1,)), ((), ())), preferred_element_type=F32)


def _rep(x, n):
    return x if n == 1 else jnp.concatenate([x] * n, axis=1)


def _rmsnorm_kernel(x_ref, g_ref, o_ref):
    x = x_ref[...]
    ms = jnp.mean(x * x, axis=-1, keepdims=True)
    o_ref[...] = (x * lax.rsqrt(ms + EPS) * g_ref[...]).astype(o_ref.dtype)


def rmsnorm(x, g, out_dtype):
    s, d = x.shape
    tr = _pick(s, (256, 128, 64, 8))
    return pl.pallas_call(
        _rmsnorm_kernel,
        grid=(s // tr,),
        in_specs=[pl.BlockSpec((tr, d), lambda i: (i, 0)),
                  pl.BlockSpec((1, d), lambda i: (0, 0))],
        out_specs=pl.BlockSpec((tr, d), lambda i: (i, 0)),
        out_shape=jax.ShapeDtypeStruct((s, d), out_dtype),
        compiler_params=_params(1),
        name="rmsnorm",
    )(x, g.reshape(1, d).astype(F32))


def _cast_stats_kernel(x_ref, g_ref, xb_ref, rstd_ref):
    x = x_ref[...]
    xb_ref[...] = (x * g_ref[...]).astype(BF16)
    ms = jnp.mean(x * x, axis=-1, keepdims=True)
    rstd_ref[...] = jnp.broadcast_to(lax.rsqrt(ms + EPS), rstd_ref.shape)


def cast_stats(x, gain):
    s, d = x.shape
    tr = _pick(s, (256, 128, 64, 8))
    return pl.pallas_call(
        _cast_stats_kernel,
        grid=(s // tr,),
        in_specs=[pl.BlockSpec((tr, d), lambda i: (i, 0)),
                  pl.BlockSpec((1, d), lambda i: (0, 0))],
        out_specs=[pl.BlockSpec((tr, d), lambda i: (i, 0)),
                   pl.BlockSpec((tr, LANES), lambda i: (i, 0))],
        out_shape=[jax.ShapeDtypeStruct((s, d), BF16),
                   jax.ShapeDtypeStruct((s, LANES), F32)],
        compiler_params=_params(1),
        name="cast_stats",
    )(x, gain.reshape(1, d).astype(F32))


def _row_chunks(n_rows):
    rows = min(EPILOGUE_ROWS, n_rows)
    assert n_rows % rows == 0
    return [slice(r, r + rows) for r in range(0, n_rows, rows)]


def _gateup_kernel(a_ref, rstd_ref, w1_ref, w3_ref, w2_ref, o_ref, w2b_ref):
    w1 = w1_ref[...].astype(BF16)
    w3 = w3_ref[...].astype(BF16)
    for sl in _row_chunks(o_ref.shape[0]):
        a = a_ref[sl, :]
        rstd = _rep(rstd_ref[sl, :], o_ref.shape[1] // LANES)
        g = rstd * _dot(a, w1)
        u = rstd * _dot(a, w3)
        o_ref[sl, :] = (g * jax.nn.sigmoid(g) * u).astype(o_ref.dtype)
    w2b_ref[...] = w2_ref[...].astype(BF16)


def gateup(xb, rstd, w1, w3, w2):
    m, k = xb.shape
    n = w1.shape[1]
    k2, n2 = w2.shape
    tm = _pick(m, (2048, 1024, 512, 256, 128, 64, 8))
    tn = _pick(n, (256, 128))
    n_i, n_j = m // tm, n // tn
    slab = k2 // (n_i * n_j)
    assert slab * n_i * n_j == k2 and slab % BF16_SUBLANES == 0, (k2, n_i, n_j)
    return pl.pallas_call(
        _gateup_kernel,
        grid=(n_i, n_j),
        in_specs=[pl.BlockSpec((tm, k), lambda i, j: (i, 0), pipeline_mode=pl.Buffered(1)),
                  pl.BlockSpec((tm, LANES), lambda i, j: (i, 0)),
                  pl.BlockSpec((k, tn), lambda i, j: (0, j)),
                  pl.BlockSpec((k, tn), lambda i, j: (0, j)),
                  pl.BlockSpec((slab, n2), lambda i, j: (i * n_j + j, 0))],
        out_specs=[pl.BlockSpec((tm, tn), lambda i, j: (i, j)),
                   pl.BlockSpec((slab, n2), lambda i, j: (i * n_j + j, 0))],
        out_shape=[jax.ShapeDtypeStruct((m, n), BF16),
                   jax.ShapeDtypeStruct((k2, n2), BF16)],
        compiler_params=_params(2),
        name="ffn_gateup",
    )(xb, rstd, w1, w3, w2)


def _resid_kernel(a_ref, w_ref, x_ref, o_ref, *, alpha):
    w = w_ref[...].astype(BF16)
    for sl in _row_chunks(o_ref.shape[0]):
        o_ref[sl, :] = x_ref[sl, :] + alpha * _dot(a_ref[sl, :], w)


def _resid_stats_kernel(a_ref, w_ref, x_ref, g_ref, o_ref, ob_ref, rstd_ref, ssq_ref, *, alpha, n_total):
    j = pl.program_id(1)
    w = w_ref[...].astype(BF16)
    parts = []
    for sl in _row_chunks(o_ref.shape[0]):
        y = x_ref[sl, :] + alpha * _dot(a_ref[sl, :], w)
        o_ref[sl, :] = y
        ob_ref[sl, :] = (y * g_ref[...]).astype(BF16)
        sq = y * y
        part = sq[:, :LANES]
        for t in range(1, sq.shape[1] // LANES):
            part = part + sq[:, t * LANES:(t + 1) * LANES]
        parts.append(part)
    part = jnp.concatenate(parts, axis=0)

    @pl.when(j == 0)
    def _():
        ssq_ref[...] = part

    @pl.when(j > 0)
    def _():
        ssq_ref[...] += part

    @pl.when(j == pl.num_programs(1) - 1)
    def _():
        ms = jnp.sum(ssq_ref[...], axis=-1, keepdims=True) * (1.0 / n_total)
        rstd_ref[...] = jnp.broadcast_to(lax.rsqrt(ms + EPS), rstd_ref.shape)


def matmul_residual(a, w, x, alpha, tm_max, tn_max, next_gain=None):
    m, k = a.shape
    n = w.shape[1]
    tm = _pick(m, tuple(t for t in (1024, 512, 256, 128, 64, 8) if t <= tm_max))
    tn = _pick(n, tuple(t for t in (512, 256, 128) if t <= tn_max))
    tile = pl.BlockSpec((tm, tn), lambda i, j: (i, j))
    in_specs = [pl.BlockSpec((tm, k), lambda i, j: (i, 0)),
                pl.BlockSpec((k, tn), lambda i, j: (0, j)),
                tile]
    if next_gain is None:
        return pl.pallas_call(
            functools.partial(_resid_kernel, alpha=alpha),
            grid=(m // tm, n // tn),
            in_specs=in_specs,
            out_specs=tile,
            out_shape=jax.ShapeDtypeStruct((m, n), F32),
            compiler_params=_params(2),
            name="matmul_residual",
        )(a, w, x)
    return pl.pallas_call(
        functools.partial(_resid_stats_kernel, alpha=alpha, n_total=n),
        grid=(m // tm, n // tn),
        in_specs=in_specs + [pl.BlockSpec((1, tn), lambda i, j: (0, j))],
        out_specs=[tile, tile, pl.BlockSpec((tm, LANES), lambda i, j: (i, 0))],
        out_shape=[jax.ShapeDtypeStruct((m, n), F32),
                   jax.ShapeDtypeStruct((m, n), BF16),
                   jax.ShapeDtypeStruct((m, LANES), F32)],
        scratch_shapes=[pltpu.VMEM((tm, LANES), F32)],
        compiler_params=_params(2),
        name="matmul_residual_stats",
    )(a, w, x, next_gain.reshape(1, n).astype(F32))


def _proj_kernel(a_ref, rstd_ref, wt_ref, o_ref):
    wt = wt_ref[...].astype(BF16)
    for sl in _row_chunks(o_ref.shape[0]):
        y = _dot_nt(a_ref[sl, :], wt)
        o_ref[sl, :] = (_rep(rstd_ref[sl, :], y.shape[1] // LANES) * y).astype(o_ref.dtype)


def _proj_scaled_kernel(a_ref, rstd_ref, wt_ref, s_ref, *refs):
    n_jobs = (len(refs) - 1) // 2
    o_ref = refs[n_jobs]
    wt = wt_ref[...].astype(BF16)
    for sl in _row_chunks(o_ref.shape[0]):
        y = _dot_nt(a_ref[sl, :], wt)
        o_ref[sl, :] = (_rep(rstd_ref[sl, :], y.shape[1] // LANES) * y * s_ref[...]).astype(o_ref.dtype)
    for src_ref, dst_ref in zip(refs[:n_jobs], refs[n_jobs + 1:]):
        dst_ref[...] = src_ref[...].astype(BF16)


def project(xb, rstd, wt, segments, out_dtype, col_scale=None, cast_jobs=()):
    m, k = xb.shape
    n = sum(seg_n for _, seg_n in segments)
    tm = _pick(m, (2048, 1024, 512, 256, 128, 64, 8))
    tn = 512 if all(seg_n % 512 == 0 for _, seg_n in segments) else 256
    assert all(seg_n % tn == 0 and row0 % 8 == 0 for row0, seg_n in segments)

    def wt_row(j):
        blk0 = 0
        row8 = None
        for row0, seg_n in segments:
            here = row0 // 8 + (j - blk0) * (tn // 8)
            row8 = here if row8 is None else jnp.where(j >= blk0, here, row8)
            blk0 += seg_n // tn
        return row8 * 8

    in_specs = [pl.BlockSpec((tm, k), lambda i, j: (i, 0), pipeline_mode=pl.Buffered(1)),
                pl.BlockSpec((tm, LANES), lambda i, j: (i, 0)),
                pl.BlockSpec((pl.Element(tn), pl.Element(k)), lambda i, j: (wt_row(j), 0))]
    args = [xb, rstd, wt]
    body = _proj_kernel
    if col_scale is not None:
        in_specs.append(pl.BlockSpec((1, tn), lambda i, j: (0, j)))
        args.append(col_scale.reshape(1, n).astype(F32))
        body = _proj_scaled_kernel
    out_specs = [pl.BlockSpec((tm, tn), lambda i, j: (i, j))]
    out_shape = [jax.ShapeDtypeStruct((m, n), out_dtype)]
    n_i, n_j = m // tm, n // tn
    assert not cast_jobs or col_scale is not None
    for w in cast_jobs:
        rows, cols = w.shape
        slab = rows // (n_i * n_j)
        assert slab * n_i * n_j == rows and slab % BF16_SUBLANES == 0, (rows, n_i, n_j)
        slab_spec = pl.BlockSpec((slab, cols), lambda i, j: (i * n_j + j, 0))
        in_specs.append(slab_spec)
        args.append(w)
        out_specs.append(slab_spec)
        out_shape.append(jax.ShapeDtypeStruct((rows, cols), BF16))
    res = pl.pallas_call(
        body,
        grid=(n_i, n_j),
        in_specs=in_specs,
        out_specs=out_specs,
        out_shape=out_shape,
        compiler_params=_params(2),
        name="in_project",
    )(*args)
    return tuple(res) if cast_jobs else res[0]


def _merge_kernel(ya_ref, yb_ref, wa_ref, wb_ref, ga_ref, gb_ref, o_ref):
    wa = wa_ref[...].astype(BF16)
    wb = wb_ref[...].astype(BF16)
    for sl in _row_chunks(o_ref.shape[0]):
        pa = _dot(ya_ref[sl, :], wa)
        pb = _dot(yb_ref[sl, :], wb)
        o_ref[sl, :] = (jax.nn.sigmoid(ga_ref[sl, :]) * pa
                        + jax.nn.sigmoid(gb_ref[sl, :]) * pb).astype(o_ref.dtype)


def gated_merge(ya, yb, wa, wb, gates, ga_col0, gb_col0):
    m, k = ya.shape
    n = wa.shape[1]
    tm = _pick(m, (1024, 512, 256, 128, 64, 8))
    tn = _pick(n, (512, 256, 128))
    assert ga_col0 % tn == 0 and gb_col0 % tn == 0
    ga_blk, gb_blk = ga_col0 // tn, gb_col0 // tn
    return pl.pallas_call(
        _merge_kernel,
        grid=(m // tm, n // tn),
        in_specs=[pl.BlockSpec((tm, k), lambda i, j: (i, 0)),
                  pl.BlockSpec((tm, k), lambda i, j: (i, 0)),
                  pl.BlockSpec((k, tn), lambda i, j: (0, j)),
                  pl.BlockSpec((k, tn), lambda i, j: (0, j)),
                  pl.BlockSpec((tm, tn), lambda i, j: (i, ga_blk + j)),
                  pl.BlockSpec((tm, tn), lambda i, j: (i, gb_blk + j))],
        out_specs=pl.BlockSpec((tm, tn), lambda i, j: (i, j)),
        out_shape=jax.ShapeDtypeStruct((m, n), BF16),
        compiler_params=_params(2),
        name="gated_merge",
    )(ya, yb, wa, wb, gates, gates)


def _conv_kernel(cur_ref, prev_ref, w_ref, q_ref, k_ref, *, k_scale):
    cur = cur_ref[...]
    prev = jnp.where(pl.program_id(0) > 0, prev_ref[...], 0.0)
    ext = jnp.concatenate([prev, cur], axis=0)
    w = w_ref[...]
    tr = cur.shape[0]
    acc = w[CONV_WIDTH - 1:CONV_WIDTH] * cur
    for j in range(CONV_WIDTH - 1):
        off = 8 - (CONV_WIDTH - 1) + j
        acc = acc + w[j:j + 1] * ext[off:off + tr]
    y = acc * jax.nn.sigmoid(acc)
    half = y.shape[1] // 2
    q_ref[...] = y[:, :half].astype(q_ref.dtype)
    k_ref[...] = (y[:, half:] * k_scale).astype(k_ref.dtype)


def conv_silu_qk(p, width2, conv_w, k_scale):
    s = p.shape[0]
    tr = _pick(s, (256, 128, 64, 8))
    half = width2 // 2
    return pl.pallas_call(
        functools.partial(_conv_kernel, k_scale=k_scale),
        grid=(s // tr,),
        in_specs=[pl.BlockSpec((tr, width2), lambda i: (i, 0)),
                  pl.BlockSpec((8, width2), lambda i: (jnp.maximum(i * (tr // 8) - 1, 0), 0)),
                  pl.BlockSpec((CONV_WIDTH, width2), lambda i: (0, 0))],
        out_specs=[pl.BlockSpec((tr, half), lambda i: (i, 0)),
                   pl.BlockSpec((tr, half), lambda i: (i, 0))],
        out_shape=[jax.ShapeDtypeStruct((s, half), BF16),
                   jax.ShapeDtypeStruct((s, half), BF16)],
        compiler_params=_params(1),
        name="conv_silu_qk",
    )(p, p, conv_w.astype(F32))


def _split3(x):
    hi = x.astype(BF16)
    rem = x - hi.astype(F32)
    mid = rem.astype(BF16)
    lo = (rem - mid.astype(F32)).astype(BF16)
    return jnp.concatenate([hi, mid, lo], axis=1)


def _gate_kernel(g_ref, bi_ref, bf_ref, e_ref, r_ref, brep_ref, rmaxrep_ref, arep_ref, *, chunk):
    g = g_ref[...]
    rows = g.shape[0]
    li = g[:, :LANES] + bi_ref[...]
    lf = _log_sigmoid(g[:, LANES:] + bf_ref[...])
    pos = lax.broadcasted_iota(jnp.int32, lf.shape, 0) % chunk

    def scan(x, op, fill, down):
        d = 1
        while d < chunk:
            if down:
                x = op(x, jnp.where(pos >= d, pltpu.roll(x, d, 0), fill))
            else:
                x = op(x, jnp.where(pos + d < chunk, pltpu.roll(x, rows - d, 0), fill))
            d *= 2
        return x

    b = scan(lf, jnp.add, 0.0, True)
    r = li - b
    rmax = scan(r, jnp.maximum, -jnp.inf, True)
    a = (scan(lf, jnp.add, 0.0, False) - lf) + li
    r_ref[...] = r
    for x, ref in ((b, brep_ref), (rmax, rmaxrep_ref), (a, arep_ref)):
        ref[...] = _dot(_split3(x), e_ref[...])


def gate_prep(g, b_i, b_f, chunk, heads):
    s = g.shape[0]
    tr = _pick(s, (512, 256, 128, 64))
    pad = lambda v: jnp.zeros((1, LANES), F32).at[0, :v.shape[0]].set(v.astype(F32))
    src_lane = lax.broadcasted_iota(jnp.int32, (3 * LANES, heads * LANES), 0) % LANES
    dst_head = lax.broadcasted_iota(jnp.int32, (3 * LANES, heads * LANES), 1) // LANES
    expand = (src_lane == dst_head).astype(BF16)
    rep = jax.ShapeDtypeStruct((s, heads * LANES), F32)
    return pl.pallas_call(
        functools.partial(_gate_kernel, chunk=chunk),
        grid=(s // tr,),
        in_specs=[pl.BlockSpec((tr, 2 * LANES), lambda i: (i, 0)),
                  pl.BlockSpec((1, LANES), lambda i: (0, 0)),
                  pl.BlockSpec((1, LANES), lambda i: (0, 0)),
                  pl.BlockSpec((3 * LANES, heads * LANES), lambda i: (0, 0))],
        out_specs=[pl.BlockSpec((tr, LANES), lambda i: (i, 0))]
                  + [pl.BlockSpec((tr, heads * LANES), lambda i: (i, 0))] * 3,
        out_shape=[jax.ShapeDtypeStruct((s, LANES), F32), rep, rep, rep],
        compiler_params=_params(1),
        name="gate_prep",
    )(g, pad(b_i), pad(b_f), expand)


def _mlstm_kernel(q_ref, k_ref, v_ref, mo_ref, brep_ref, rmaxrep_ref, arep_ref, r_ref, g_ref, o_ref,
                  cn_ref, m_ref, *, heads, dk, dv, chunk, n_chunks):
    @pl.when(pl.program_id(0) == 0)
    def _():
        cn_ref[...] = jnp.zeros_like(cn_ref)
        m_ref[...] = jnp.zeros_like(m_ref)

    row = lax.broadcasted_iota(jnp.int32, (chunk, chunk), 0)
    col = lax.broadcasted_iota(jnp.int32, (chunk, chunk), 1)
    tril = row >= col
    ones_ext = jnp.ones((chunk, LANES), BF16)
    ones_sum = jnp.ones((2 * dv, LANES), BF16)
    v_tiles = dv // LANES
    rep = _rep

    def lanes(h):
        return slice(h * LANES, (h + 1) * LANES)

    def chunk_body(c, carry):
        r0 = pl.multiple_of(c * chunk, chunk)
        rows = pl.ds(r0, chunk)
        tail = pl.ds(pl.multiple_of(r0 + chunk - 8, 8), 8)
        r_all = r_ref[c]
        hs = range(heads)
        q = [q_ref[rows, h * dk:(h + 1) * dk] for h in hs]
        k = [k_ref[rows, h * dk:(h + 1) * dk] for h in hs]
        v_ext = [jnp.concatenate([v_ref[rows, h * dv:(h + 1) * dv], ones_ext], axis=1) for h in hs]
        cn = [cn_ref[h] for h in hs]
        s = [_dot_nt(q[h], k[h]) for h in hs]
        inter = [_dot(q[h], cn[h].astype(BF16)) for h in hs]

        m_prev = [m_ref[h:h + 1, :] for h in hs]
        big_m = [jnp.maximum(m_prev[h], rmaxrep_ref[rows, lanes(h)]) for h in hs]
        w = []
        for h in hs:
            d_exp = jnp.exp(r_all[h:h + 1, :] - big_m[h][:, :chunk])
            w.append((s[h] * jnp.where(tril, d_exp, 0.0)).astype(BF16))
        intra = [_dot(w[h], v_ext[h]) for h in hs]

        for h in hs:
            decay = jnp.exp(m_prev[h] - big_m[h])
            tot = rep(decay, v_tiles + 1) * inter[h] + intra[h]
            floor = jnp.exp(-(brep_ref[rows, lanes(h)] + big_m[h]))
            denom = jnp.maximum(jnp.abs(tot[:, dv:]), floor)
            hh = tot[:, :dv] / rep(denom, v_tiles)
            sq = hh * hh
            sq_hi = sq.astype(BF16)
            sq_lo = (sq - sq_hi.astype(F32)).astype(BF16)
            ms = _dot(jnp.concatenate([sq_hi, sq_lo], axis=1), ones_sum) * (1.0 / dv)
            y = hh * rep(lax.rsqrt(ms + EPS), v_tiles) * g_ref[:, h * dv:(h + 1) * dv]
            y = y * jax.nn.sigmoid(mo_ref[rows, h * dv:(h + 1) * dv])
            o_ref[rows, h * dv:(h + 1) * dv] = y.astype(o_ref.dtype)

        sv, m_new, carry_scale = [], [], []
        for h in hs:
            a_rep = arep_ref[rows, lanes(h)]
            b_last = brep_ref[tail, lanes(h)][7:8, :]
            m_new.append(jnp.maximum(b_last + m_prev[h], jnp.max(a_rep, axis=0, keepdims=True)))
            carry_scale.append(jnp.exp(b_last + m_prev[h] - m_new[h]))
            src = jnp.exp(a_rep - m_new[h])
            sv.append((rep(src, v_tiles + 1) * v_ext[h].astype(F32)).astype(BF16))
        upd = [lax.dot_general(k[h], sv[h], (((0,), (0,)), ((), ())), preferred_element_type=F32)
               for h in hs]
        for h in hs:
            cn_ref[h] = rep(carry_scale[h], v_tiles + 1) * cn[h] + upd[h]
            m_ref[h:h + 1, :] = m_new[h]
        return carry

    lax.fori_loop(0, n_chunks, chunk_body, 0)


def mlstm(q, k, v, v_col0, mo, mo_col0, b_rep, rmax_rep, a_rep, r_rows, g_out, heads, dk, dv, chunk):
    s = q.shape[0]
    rows = _pick(s, (512, 256, 128, 64))
    n_chunks = rows // chunk
    wv = heads * dv
    assert v_col0 % wv == 0 and mo_col0 % wv == 0 and dv % LANES == 0
    v_blk, mo_blk = v_col0 // wv, mo_col0 // wv
    rep_spec = pl.BlockSpec((rows, heads * LANES), lambda i: (i, 0))
    return pl.pallas_call(
        functools.partial(_mlstm_kernel, heads=heads, dk=dk, dv=dv, chunk=chunk, n_chunks=n_chunks),
        grid=(s // rows,),
        in_specs=[pl.BlockSpec((rows, heads * dk), lambda i: (i, 0)),
                  pl.BlockSpec((rows, heads * dk), lambda i: (i, 0)),
                  pl.BlockSpec((rows, wv), lambda i: (i, v_blk)),
                  pl.BlockSpec((rows, wv), lambda i: (i, mo_blk)),
                  rep_spec, rep_spec, rep_spec,
                  pl.BlockSpec((n_chunks, heads, chunk), lambda i: (i, 0, 0)),
                  pl.BlockSpec((1, wv), lambda i: (0, 0))],
        out_specs=pl.BlockSpec((rows, wv), lambda i: (i, 0)),
        out_shape=jax.ShapeDtypeStruct((s, wv), BF16),
        scratch_shapes=[pltpu.VMEM((heads, dk, dv + LANES), F32),
                        pltpu.VMEM((heads, LANES), F32)],
        compiler_params=_params(1),
        name="mlstm",
    )(q, k, v, mo, b_rep, rmax_rep, a_rep, r_rows, g_out.reshape(1, wv).astype(F32))


def _sb_kernel(q_ref, k_ref, v_ref, o_ref, acc_ref, run_ref, live_ref, *, tk, hp, d):
    i = pl.program_id(1)
    row = lax.broadcasted_iota(jnp.int32, (tk, tk), 0)
    col = lax.broadcasted_iota(jnp.int32, (tk, tk), 1)
    diag_strict = col < row
    sums_rhs = jnp.concatenate([(row > col).astype(BF16), jnp.ones((tk, tk), BF16)], axis=1)
    sums_rhs = jnp.concatenate([sums_rhs, sums_rhs], axis=0)

    def walk(tiles):
        zs = [_dot_nt(t[0], t[1]) for t in tiles]
        log_betas, splits = [], []
        for z, (_, _, _, _, mask, _) in zip(zs, tiles):
            log_beta = jnp.minimum(z, 0.0) - jnp.log(1.0 + jnp.exp(-jnp.abs(z)))
            log_rest = log_beta - z
            if mask is not None:
                log_rest = jnp.where(mask, log_rest, 0.0)
            hi = log_rest.astype(BF16)
            lo = (log_rest - hi.astype(F32)).astype(BF16)
            log_betas.append(log_beta)
            splits.append(jnp.concatenate([hi, lo], axis=1))
        sums = [_dot(x, sums_rhs) for x in splits]
        weights, runs = [], []
        for log_beta, sm, (_, _, _, run, mask, after) in zip(log_betas, sums, tiles):
            if after is not None:
                run = runs[after]
            log_w = log_beta + sm[:, :tk]
            wts = jnp.exp(log_w if run is None else log_w + run)
            if mask is not None:
                wts = jnp.where(mask, wts, 0.0)
            weights.append(wts.astype(BF16))
            runs.append(sm[:, tk:] if run is None else run + sm[:, tk:])
        outs = [_dot(w, t[2]) for w, t in zip(weights, tiles)]
        return list(zip(outs, runs))

    def cols(h):
        return slice(h * d, (h + 1) * d)

    def chain(near, length, run_of):
        tiles = []
        for t in range(length):
            blk = near - t
            key_rows = pl.ds(pl.multiple_of(jnp.maximum(blk, 0) * tk, tk), tk)
            exists = jnp.broadcast_to(blk >= 0, (tk, tk))
            for h in range(hp):
                kk, vv = k_ref[key_rows, cols(h)], v_ref[key_rows, cols(h)]
                if t > 0:
                    tiles.append((tiles[h][0], kk, vv, None, exists, (t - 1) * hp + h))
                elif run_of is None:
                    tiles.append((q_ref[:, cols(h)], kk, vv, None, diag_strict, None))
                else:
                    tiles.append((q_ref[:, cols(h)], kk, vv, run_of(h), None, None))
        res = walk(tiles)
        return [(functools.reduce(jnp.add, [res[t * hp + h][0] for t in range(length)]),
                 res[(length - 1) * hp + h][1]) for h in range(hp)]

    for h, (out, run) in enumerate(chain(i, SB_STATIC_BLOCKS, None)):
        acc_ref[h] = out
        run_ref[h] = run
    live_ref[0] = (jnp.max(run_ref[...]) > SB_DONE_LOG).astype(jnp.int32)

    first = i - SB_STATIC_BLOCKS
    n_pairs = (jnp.maximum(first, -1) + 2) // 2

    def pair_body(step, carry):
        @pl.when(live_ref[0] > 0)
        def _():
            for h, (out, run) in enumerate(chain(first - 2 * step, 2, lambda h: run_ref[h])):
                acc_ref[h] += out
                run_ref[h] = run
            live_ref[0] = (jnp.max(run_ref[...]) > SB_DONE_LOG).astype(jnp.int32)
        return carry

    lax.fori_loop(0, n_pairs, pair_body, 0)
    for h in range(hp):
        o_ref[:, cols(h)] = acc_ref[h].astype(o_ref.dtype)


def stick_breaking(qkv, col0, heads, d):
    s = qkv.shape[0]
    tq = tk = SB_Q_ROWS
    hp = SB_HEADS_PER_STEP
    assert s % tq == 0 and heads % hp == 0 and col0 % (hp * d) == 0
    n_hg = heads // hp
    blk0 = col0 // (hp * d)
    once = pl.Buffered(1)
    return pl.pallas_call(
        functools.partial(_sb_kernel, tk=tk, hp=hp, d=d),
        grid=(n_hg, s // tq),
        in_specs=[pl.BlockSpec((tq, hp * d), lambda hg, i: (i, blk0 + hg)),
                  pl.BlockSpec((s, hp * d), lambda hg, i: (0, blk0 + n_hg + hg), pipeline_mode=once),
                  pl.BlockSpec((s, hp * d), lambda hg, i: (0, blk0 + 2 * n_hg + hg), pipeline_mode=once)],
        out_specs=pl.BlockSpec((tq, hp * d), lambda hg, i: (i, hg)),
        out_shape=jax.ShapeDtypeStruct((s, heads * d), BF16),
        scratch_shapes=[pltpu.VMEM((hp, tq, d), F32),
                        pltpu.VMEM((hp, tq, tk), F32),
                        pltpu.SMEM((1,), jnp.int32)],
        compiler_params=_params(2),
        name="stick_breaking",
    )(qkv, qkv, qkv)


def _ffn(x, xb, rstd, w1, w3, w2, next_gain):
    u, w2b = gateup(xb, rstd, w1, w3, w2)
    return matmul_residual(u, w2b, x, 0.5, tm_max=512, tn_max=512, next_gain=next_gain)


def _mixers(x, xb, rstd, w_in, conv_qk, b_igate, b_fgate, g_mlstm_out, w_proj_a, w_proj_b, w_out, next_gain):
    s, d_model = x.shape
    qk_w = ML_HEADS * ML_QK_DIM
    ml_w = ML_HEADS * ML_V_DIM
    sb_w = SB_HEADS * SB_HEAD_DIM
    o_mv = 2 * qk_w
    o_mo = o_mv + ml_w
    o_mi = o_mo + ml_w
    o_mf = o_mi + ML_HEADS
    o_sq = o_mf + ML_HEADS
    o_ga = o_sq + 3 * sb_w
    wt = jnp.swapaxes(w_in, 0, 1)
    zrows = jnp.zeros((LANES - ML_HEADS, d_model), F32)
    wt_gate = jnp.concatenate([wt[o_mi:o_mf], zrows, wt[o_mf:o_sq], zrows], axis=0)
    scale16 = jnp.concatenate([jnp.ones((ml_w,), F32),
                               jnp.full((sb_w,), SB_HEAD_DIM ** -0.5, F32),
                               jnp.ones((2 * sb_w,), F32)])

    p32 = project(xb, rstd, wt, [(0, o_mv), (o_mo, ml_w), (o_ga, 2 * d_model)], F32)
    p16, w_proj_a, w_proj_b, w_out = project(xb, rstd, wt, [(o_mv, ml_w), (o_sq, 3 * sb_w)], BF16,
                                             col_scale=scale16, cast_jobs=(w_proj_a, w_proj_b, w_out))
    pg = project(xb, rstd, wt_gate, [(0, 2 * LANES)], F32)

    q, k = conv_silu_qk(p32, 2 * qk_w, conv_qk, ML_QK_DIM ** -0.5)
    r, b_rep, rmax_rep, a_rep = gate_prep(pg, b_igate, b_fgate, ML_CHUNK, ML_HEADS)
    r_rows = jnp.swapaxes(r[:, :ML_HEADS].reshape(s // ML_CHUNK, ML_CHUNK, ML_HEADS), 1, 2)
    ya = mlstm(q, k, p16, 0, p32, 2 * qk_w, b_rep, rmax_rep, a_rep, r_rows, g_mlstm_out,
               ML_HEADS, ML_QK_DIM, ML_V_DIM, ML_CHUNK)

    yb = stick_breaking(p16, ml_w, SB_HEADS, SB_HEAD_DIM)

    merged = gated_merge(ya, yb, w_proj_a, w_proj_b, p32, 2 * qk_w + ml_w, 2 * qk_w + ml_w + d_model)
    return matmul_residual(merged, w_out, x, 1.0, tm_max=1024, tn_max=512, next_gain=next_gain)


def kernel(x, g_ffn1, w1_ffn1, w3_ffn1, w2_ffn1, g_mix, w_in, conv_qk, b_igate, b_fgate,
           g_mlstm_out, w_proj_a, w_proj_b, w_out, g_ffn2, w1_ffn2, w3_ffn2, w2_ffn2, g_final):
    batch, seq, d_model = x.shape
    outs = []
    depth = g_ffn1.shape[0]
    for bi in range(batch):
        xf = x[bi]
        xb, rstd = cast_stats(xf, g_ffn1[0])
        for l in range(depth):
            xf, xb, rstd = _ffn(xf, xb, rstd, w1_ffn1[l], w3_ffn1[l], w2_ffn1[l], g_mix[l])
            xf, xb, rstd = _mixers(xf, xb, rstd, w_in[l], conv_qk[l], b_igate[l], b_fgate[l],
                                   g_mlstm_out[l], w_proj_a[l], w_proj_b[l], w_out[l], g_ffn2[l])
            if l + 1 < depth:
                xf, xb, rstd = _ffn(xf, xb, rstd, w1_ffn2[l], w3_ffn2[l], w2_ffn2[l], g_ffn1[l + 1])
            else:
                xf = _ffn(xf, xb, rstd, w1_ffn2[l], w3_ffn2[l], w2_ffn2[l], None)
        outs.append(rmsnorm(xf, g_final, x.dtype))
    return jnp.stack(outs, axis=0)
```

```python
import functools

import jax
import jax.numpy as jnp
from jax import lax
from jax.experimental import pallas as pl
from jax.experimental.pallas import tpu as pltpu

F32 = jnp.float32
BF16 = jnp.bfloat16
EPS = 1e-6

V7X_VMEM_BYTES = 64 * 1024 * 1024
VMEM_LIMIT_BYTES = V7X_VMEM_BYTES - 8 * 1024 * 1024
LANES = 128
BF16_SUBLANES = 16

ML_HEADS = 8
ML_QK_DIM = 128
ML_V_DIM = 256
ML_CHUNK = 64
SB_HEADS = 16
SB_HEAD_DIM = 128
EPILOGUE_ROWS = 256
SB_HEADS_PER_STEP = 8
SB_Q_ROWS = 128
SB_STATIC_BLOCKS = 3
CONV_WIDTH = 4
SB_DONE_LOG = -88.0


def _params(n_grid):
    return pltpu.CompilerParams(
        dimension_semantics=("arbitrary",) * n_grid,
        vmem_limit_bytes=VMEM_LIMIT_BYTES)


def _pick(n, candidates):
    for c in candidates:
        if n % c == 0:
            return c
    raise ValueError(f"no tile in {candidates} divides {n}")


def _log_sigmoid(x):
    return jnp.minimum(x, 0.0) - jnp.log1p(jnp.exp(-jnp.abs(x)))


def _dot(a, b):
    return jnp.dot(a, b, preferred_element_type=F32)


def _dot_nt(a, b):
    return lax.dot_general(a, b, (((1,), (1,)), ((), ())), preferred_element_type=F32)


def _rep(x, n):
    return x if n == 1 else jnp.concatenate([x] * n, axis=1)


def _rmsnorm_kernel(x_ref, g_ref, o_ref):
    x = x_ref[...]
    ms = jnp.mean(x * x, axis=-1, keepdims=True)
    o_ref[...] = (x * lax.rsqrt(ms + EPS) * g_ref[...]).astype(o_ref.dtype)


def rmsnorm(x, g, out_dtype):
    s, d = x.shape
    tr = _pick(s, (256, 128, 64, 8))
    return pl.pallas_call(
        _rmsnorm_kernel,
        grid=(s // tr,),
        in_specs=[pl.BlockSpec((tr, d), lambda i: (i, 0)),
                  pl.BlockSpec((1, d), lambda i: (0, 0))],
        out_specs=pl.BlockSpec((tr, d), lambda i: (i, 0)),
        out_shape=jax.ShapeDtypeStruct((s, d), out_dtype),
        compiler_params=_params(1),
        name="rmsnorm",
    )(x, g.reshape(1, d).astype(F32))


def _cast_stats_kernel(x_ref, g_ref, xb_ref, rstd_ref):
    x = x_ref[...]
    xb_ref[...] = (x * g_ref[...]).astype(BF16)
    ms = jnp.mean(x * x, axis=-1, keepdims=True)
    rstd_ref[...] = jnp.broadcast_to(lax.rsqrt(ms + EPS), rstd_ref.shape)


def cast_stats(x, gain):
    s, d = x.shape
    tr = _pick(s, (256, 128, 64, 8))
    return pl.pallas_call(
        _cast_stats_kernel,
        grid=(s // tr,),
        in_specs=[pl.BlockSpec((tr, d), lambda i: (i, 0)),
                  pl.BlockSpec((1, d), lambda i: (0, 0))],
        out_specs=[pl.BlockSpec((tr, d), lambda i: (i, 0)),
                   pl.BlockSpec((tr, LANES), lambda i: (i, 0))],
        out_shape=[jax.ShapeDtypeStruct((s, d), BF16),
                   jax.ShapeDtypeStruct((s, LANES), F32)],
        compiler_params=_params(1),
        name="cast_stats",
    )(x, gain.reshape(1, d).astype(F32))


def _row_chunks(n_rows):
    rows = min(EPILOGUE_ROWS, n_rows)
    assert n_rows % rows == 0
    return [slice(r, r + rows) for r in range(0, n_rows, rows)]


def _gateup_kernel(a_ref, rstd_ref, w1_ref, w3_ref, w2_ref, o_ref, w2b_ref):
    w1 = w1_ref[...].astype(BF16)
    w3 = w3_ref[...].astype(BF16)
    for sl in _row_chunks(o_ref.shape[0]):
        a = a_ref[sl, :]
        rstd = _rep(rstd_ref[sl, :], o_ref.shape[1] // LANES)
        g = rstd * _dot(a, w1)
        u = rstd * _dot(a, w3)
        o_ref[sl, :] = (g * jax.nn.sigmoid(g) * u).astype(o_ref.dtype)
    w2b_ref[...] = w2_ref[...].astype(BF16)


def gateup(xb, rstd, w1, w3, w2):
    m, k = xb.shape
    n = w1.shape[1]
    k2, n2 = w2.shape
    tm = _pick(m, (2048, 1024, 512, 256, 128, 64, 8))
    tn = _pick(n, (256, 128))
    n_i, n_j = m // tm, n // tn
    slab = k2 // (n_i * n_j)
    assert slab * n_i * n_j == k2 and slab % BF16_SUBLANES == 0, (k2, n_i, n_j)
    return pl.pallas_call(
        _gateup_kernel,
        grid=(n_i, n_j),
        in_specs=[pl.BlockSpec((tm, k), lambda i, j: (i, 0), pipeline_mode=pl.Buffered(1)),
                  pl.BlockSpec((tm, LANES), lambda i, j: (i, 0)),
                  pl.BlockSpec((k, tn), lambda i, j: (0, j)),
                  pl.BlockSpec((k, tn), lambda i, j: (0, j)),
                  pl.BlockSpec((slab, n2), lambda i, j: (i * n_j + j, 0))],
        out_specs=[pl.BlockSpec((tm, tn), lambda i, j: (i, j)),
                   pl.BlockSpec((slab, n2), lambda i, j: (i * n_j + j, 0))],
        out_shape=[jax.ShapeDtypeStruct((m, n), BF16),
                   jax.ShapeDtypeStruct((k2, n2), BF16)],
        compiler_params=_params(2),
        name="ffn_gateup",
    )(xb, rstd, w1, w3, w2)


def _resid_kernel(a_ref, w_ref, x_ref, o_ref, *, alpha):
    w = w_ref[...].astype(BF16)
    for sl in _row_chunks(o_ref.shape[0]):
        o_ref[sl, :] = x_ref[sl, :] + alpha * _dot(a_ref[sl, :], w)


def _resid_stats_kernel(a_ref, w_ref, x_ref, g_ref, o_ref, ob_ref, rstd_ref, ssq_ref, *, alpha, n_total):
    j = pl.program_id(1)
    w = w_ref[...].astype(BF16)
    parts = []
    for sl in _row_chunks(o_ref.shape[0]):
        y = x_ref[sl, :] + alpha * _dot(a_ref[sl, :], w)
        o_ref[sl, :] = y
        ob_ref[sl, :] = (y * g_ref[...]).astype(BF16)
        sq = y * y
        part = sq[:, :LANES]
        for t in range(1, sq.shape[1] // LANES):
            part = part + sq[:, t * LANES:(t + 1) * LANES]
        parts.append(part)
    part = jnp.concatenate(parts, axis=0)

    @pl.when(j == 0)
    def _():
        ssq_ref[...] = part

    @pl.when(j > 0)
    def _():
        ssq_ref[...] += part

    @pl.when(j == pl.num_programs(1) - 1)
    def _():
        ms = jnp.sum(ssq_ref[...], axis=-1, keepdims=True) * (1.0 / n_total)
        rstd_ref[...] = jnp.broadcast_to(lax.rsqrt(ms + EPS), rstd_ref.shape)


def matmul_residual(a, w, x, alpha, tm_max, tn_max, next_gain=None):
    m, k = a.shape
    n = w.shape[1]
    tm = _pick(m, tuple(t for t in (1024, 512, 256, 128, 64, 8) if t <= tm_max))
    tn = _pick(n, tuple(t for t in (512, 256, 128) if t <= tn_max))
    tile = pl.BlockSpec((tm, tn), lambda i, j: (i, j))
    in_specs = [pl.BlockSpec((tm, k), lambda i, j: (i, 0)),
                pl.BlockSpec((k, tn), lambda i, j: (0, j)),
                tile]
    if next_gain is None:
        return pl.pallas_call(
            functools.partial(_resid_kernel, alpha=alpha),
            grid=(m // tm, n // tn),
            in_specs=in_specs,
            out_specs=tile,
            out_shape=jax.ShapeDtypeStruct((m, n), F32),
            compiler_params=_params(2),
            name="matmul_residual",
        )(a, w, x)
    return pl.pallas_call(
        functools.partial(_resid_stats_kernel, alpha=alpha, n_total=n),
        grid=(m // tm, n // tn),
        in_specs=in_specs + [pl.BlockSpec((1, tn), lambda i, j: (0, j))],
        out_specs=[tile, tile, pl.BlockSpec((tm, LANES), lambda i, j: (i, 0))],
        out_shape=[jax.ShapeDtypeStruct((m, n), F32),
                   jax.ShapeDtypeStruct((m, n), BF16),
                   jax.ShapeDtypeStruct((m, LANES), F32)],
        scratch_shapes=[pltpu.VMEM((tm, LANES), F32)],
        compiler_params=_params(2),
        name="matmul_residual_stats",
    )(a, w, x, next_gain.reshape(1, n).astype(F32))


def _proj_kernel(a_ref, rstd_ref, wt_ref, o_ref):
    wt = wt_ref[...].astype(BF16)
    for sl in _row_chunks(o_ref.shape[0]):
        y = _dot_nt(a_ref[sl, :], wt)
        o_ref[sl, :] = (_rep(rstd_ref[sl, :], y.shape[1] // LANES) * y).astype(o_ref.dtype)


def _proj_scaled_kernel(a_ref, rstd_ref, wt_ref, s_ref, *refs):
    n_jobs = (len(refs) - 1) // 2
    o_ref = refs[n_jobs]
    wt = wt_ref[...].astype(BF16)
    for sl in _row_chunks(o_ref.shape[0]):
        y = _dot_nt(a_ref[sl, :], wt)
        o_ref[sl, :] = (_rep(rstd_ref[sl, :], y.shape[1] // LANES) * y * s_ref[...]).astype(o_ref.dtype)
    for src_ref, dst_ref in zip(refs[:n_jobs], refs[n_jobs + 1:]):
        dst_ref[...] = src_ref[...].astype(BF16)


def project(xb, rstd, wt, segments, out_dtype, col_scale=None, cast_jobs=()):
    m, k = xb.shape
    n = sum(seg_n for _, seg_n in segments)
    tn = 512 if all(seg_n % 512 == 0 for _, seg_n in segments) else 256
    assert all(seg_n % tn == 0 and row0 % 8 == 0 for row0, seg_n in segments)
    if n == tn:
        tm = _pick(m, (512, 256, 128, 64, 8))
        panel = pl.BlockSpec((tm, k), lambda i, j: (i, 0))
    else:
        tm = _pick(m, (2048, 1024, 512, 256, 128, 64, 8))
        panel = pl.BlockSpec((tm, k), lambda i, j: (i, 0), pipeline_mode=pl.Buffered(1))

    def wt_row(j):
        blk0 = 0
        row8 = None
        for row0, seg_n in segments:
            here = row0 // 8 + (j - blk0) * (tn // 8)
            row8 = here if row8 is None else jnp.where(j >= blk0, here, row8)
            blk0 += seg_n // tn
        return row8 * 8

    in_specs = [panel,
                pl.BlockSpec((tm, LANES), lambda i, j: (i, 0)),
                pl.BlockSpec((pl.Element(tn), pl.Element(k)), lambda i, j: (wt_row(j), 0))]
    args = [xb, rstd, wt]
    body = _proj_kernel
    if col_scale is not None:
        in_specs.append(pl.BlockSpec((1, tn), lambda i, j: (0, j)))
        args.append(col_scale.reshape(1, n).astype(F32))
        body = _proj_scaled_kernel
    out_specs = [pl.BlockSpec((tm, tn), lambda i, j: (i, j))]
    out_shape = [jax.ShapeDtypeStruct((m, n), out_dtype)]
    n_i, n_j = m // tm, n // tn
    assert not cast_jobs or col_scale is not None
    for w in cast_jobs:
        rows, cols = w.shape
        slab = rows // (n_i * n_j)
        assert slab * n_i * n_j == rows and slab % BF16_SUBLANES == 0, (rows, n_i, n_j)
        slab_spec = pl.BlockSpec((slab, cols), lambda i, j: (i * n_j + j, 0))
        in_specs.append(slab_spec)
        args.append(w)
        out_specs.append(slab_spec)
        out_shape.append(jax.ShapeDtypeStruct((rows, cols), BF16))
    res = pl.pallas_call(
        body,
        grid=(n_i, n_j),
        in_specs=in_specs,
        out_specs=out_specs,
        out_shape=out_shape,
        compiler_params=_params(2),
        name="in_project",
    )(*args)
    return tuple(res) if cast_jobs else res[0]


def _merge_kernel(ya_ref, yb_ref, wa_ref, wb_ref, ga_ref, gb_ref, o_ref):
    wa = wa_ref[...].astype(BF16)
    wb = wb_ref[...].astype(BF16)
    for sl in _row_chunks(o_ref.shape[0]):
        pa = _dot(ya_ref[sl, :], wa)
        pb = _dot(yb_ref[sl, :], wb)
        o_ref[sl, :] = (jax.nn.sigmoid(ga_ref[sl, :]) * pa
                        + jax.nn.sigmoid(gb_ref[sl, :]) * pb).astype(o_ref.dtype)


def gated_merge(ya, yb, wa, wb, gates, ga_col0, gb_col0):
    m, k = ya.shape
    n = wa.shape[1]
    tm = _pick(m, (1024, 512, 256, 128, 64, 8))
    tn = _pick(n, (512, 256, 128))
    assert ga_col0 % tn == 0 and gb_col0 % tn == 0
    ga_blk, gb_blk = ga_col0 // tn, gb_col0 // tn
    return pl.pallas_call(
        _merge_kernel,
        grid=(m // tm, n // tn),
        in_specs=[pl.BlockSpec((tm, k), lambda i, j: (i, 0)),
                  pl.BlockSpec((tm, k), lambda i, j: (i, 0)),
                  pl.BlockSpec((k, tn), lambda i, j: (0, j)),
                  pl.BlockSpec((k, tn), lambda i, j: (0, j)),
                  pl.BlockSpec((tm, tn), lambda i, j: (i, ga_blk + j)),
                  pl.BlockSpec((tm, tn), lambda i, j: (i, gb_blk + j))],
        out_specs=pl.BlockSpec((tm, tn), lambda i, j: (i, j)),
        out_shape=jax.ShapeDtypeStruct((m, n), BF16),
        compiler_params=_params(2),
        name="gated_merge",
    )(ya, yb, wa, wb, gates, gates)


def _conv_kernel(cur_ref, prev_ref, w_ref, q_ref, k_ref, *, k_scale):
    cur = cur_ref[...]
    prev = jnp.where(pl.program_id(0) > 0, prev_ref[...], 0.0)
    ext = jnp.concatenate([prev, cur], axis=0)
    w = w_ref[...]
    tr = cur.shape[0]
    acc = w[CONV_WIDTH - 1:CONV_WIDTH] * cur
    for j in range(CONV_WIDTH - 1):
        off = 8 - (CONV_WIDTH - 1) + j
        acc = acc + w[j:j + 1] * ext[off:off + tr]
    y = acc * jax.nn.sigmoid(acc)
    half = y.shape[1] // 2
    q_ref[...] = y[:, :half].astype(q_ref.dtype)
    k_ref[...] = (y[:, half:] * k_scale).astype(k_ref.dtype)


def conv_silu_qk(p, width2, conv_w, k_scale):
    s = p.shape[0]
    tr = _pick(s, (256, 128, 64, 8))
    half = width2 // 2
    return pl.pallas_call(
        functools.partial(_conv_kernel, k_scale=k_scale),
        grid=(s // tr,),
        in_specs=[pl.BlockSpec((tr, width2), lambda i: (i, 0)),
                  pl.BlockSpec((8, width2), lambda i: (jnp.maximum(i * (tr // 8) - 1, 0), 0)),
                  pl.BlockSpec((CONV_WIDTH, width2), lambda i: (0, 0))],
        out_specs=[pl.BlockSpec((tr, half), lambda i: (i, 0)),
                   pl.BlockSpec((tr, half), lambda i: (i, 0))],
        out_shape=[jax.ShapeDtypeStruct((s, half), BF16),
                   jax.ShapeDtypeStruct((s, half), BF16)],
        compiler_params=_params(1),
        name="conv_silu_qk",
    )(p, p, conv_w.astype(F32))


def _split3(x):
    hi = x.astype(BF16)
    rem = x - hi.astype(F32)
    mid = rem.astype(BF16)
    lo = (rem - mid.astype(F32)).astype(BF16)
    return jnp.concatenate([hi, mid, lo], axis=1)


def _gate_kernel(g_ref, bi_ref, bf_ref, e_ref, r_ref, brep_ref, rmaxrep_ref, arep_ref, *, chunk):
    g = g_ref[...]
    rows = g.shape[0]
    li = g[:, :LANES] + bi_ref[...]
    lf = _log_sigmoid(g[:, LANES:] + bf_ref[...])
    pos = lax.broadcasted_iota(jnp.int32, lf.shape, 0) % chunk

    def scan(x, op, fill, down):
        d = 1
        while d < chunk:
            if down:
                x = op(x, jnp.where(pos >= d, pltpu.roll(x, d, 0), fill))
            else:
                x = op(x, jnp.where(pos + d < chunk, pltpu.roll(x, rows - d, 0), fill))
            d *= 2
        return x

    b = scan(lf, jnp.add, 0.0, True)
    r = li - b
    rmax = scan(r, jnp.maximum, -jnp.inf, True)
    a = (scan(lf, jnp.add, 0.0, False) - lf) + li
    r_ref[...] = r
    for x, ref in ((b, brep_ref), (rmax, rmaxrep_ref), (a, arep_ref)):
        ref[...] = _dot(_split3(x), e_ref[...])


def gate_prep(g, b_i, b_f, chunk, heads):
    s = g.shape[0]
    tr = _pick(s, (512, 256, 128, 64))
    pad = lambda v: jnp.zeros((1, LANES), F32).at[0, :v.shape[0]].set(v.astype(F32))
    src_lane = lax.broadcasted_iota(jnp.int32, (3 * LANES, heads * LANES), 0) % LANES
    dst_head = lax.broadcasted_iota(jnp.int32, (3 * LANES, heads * LANES), 1) // LANES
    expand = (src_lane == dst_head).astype(BF16)
    rep = jax.ShapeDtypeStruct((s, heads * LANES), F32)
    return pl.pallas_call(
        functools.partial(_gate_kernel, chunk=chunk),
        grid=(s // tr,),
        in_specs=[pl.BlockSpec((tr, 2 * LANES), lambda i: (i, 0)),
                  pl.BlockSpec((1, LANES), lambda i: (0, 0)),
                  pl.BlockSpec((1, LANES), lambda i: (0, 0)),
                  pl.BlockSpec((3 * LANES, heads * LANES), lambda i: (0, 0))],
        out_specs=[pl.BlockSpec((tr, LANES), lambda i: (i, 0))]
                  + [pl.BlockSpec((tr, heads * LANES), lambda i: (i, 0))] * 3,
        out_shape=[jax.ShapeDtypeStruct((s, LANES), F32), rep, rep, rep],
        compiler_params=_params(1),
        name="gate_prep",
    )(g, pad(b_i), pad(b_f), expand)


def _mlstm_kernel(q_ref, k_ref, v_ref, mo_ref, brep_ref, rmaxrep_ref, arep_ref, r_ref, g_ref, o_ref,
                  cn_ref, m_ref, *, heads, dk, dv, chunk, n_chunks):
    @pl.when(pl.program_id(0) == 0)
    def _():
        cn_ref[...] = jnp.zeros_like(cn_ref)
        m_ref[...] = jnp.zeros_like(m_ref)

    row = lax.broadcasted_iota(jnp.int32, (chunk, chunk), 0)
    col = lax.broadcasted_iota(jnp.int32, (chunk, chunk), 1)
    tril = row >= col
    ones_ext = jnp.ones((chunk, LANES), BF16)
    ones_sum = jnp.ones((2 * dv, LANES), BF16)
    v_tiles = dv // LANES
    rep = _rep

    def lanes(h):
        return slice(h * LANES, (h + 1) * LANES)

    def chunk_body(c, carry):
        r0 = pl.multiple_of(c * chunk, chunk)
        rows = pl.ds(r0, chunk)
        tail = pl.ds(pl.multiple_of(r0 + chunk - 8, 8), 8)
        r_all = r_ref[c]
        hs = range(heads)
        q = [q_ref[rows, h * dk:(h + 1) * dk] for h in hs]
        k = [k_ref[rows, h * dk:(h + 1) * dk] for h in hs]
        v_ext = [jnp.concatenate([v_ref[rows, h * dv:(h + 1) * dv], ones_ext], axis=1) for h in hs]
        cn = [cn_ref[h] for h in hs]
        s = [_dot_nt(q[h], k[h]) for h in hs]
        inter = [_dot(q[h], cn[h].astype(BF16)) for h in hs]

        m_prev = [m_ref[h:h + 1, :] for h in hs]
        big_m = [jnp.maximum(m_prev[h], rmaxrep_ref[rows, lanes(h)]) for h in hs]
        w = []
        for h in hs:
            d_exp = jnp.exp(r_all[h:h + 1, :] - big_m[h][:, :chunk])
            w.append((s[h] * jnp.where(tril, d_exp, 0.0)).astype(BF16))
        intra = [_dot(w[h], v_ext[h]) for h in hs]

        for h in hs:
            decay = jnp.exp(m_prev[h] - big_m[h])
            tot = rep(decay, v_tiles + 1) * inter[h] + intra[h]
            floor = jnp.exp(-(brep_ref[rows, lanes(h)] + big_m[h]))
            denom = jnp.maximum(jnp.abs(tot[:, dv:]), floor)
            hh = tot[:, :dv] / rep(denom, v_tiles)
            sq = hh * hh
            sq_hi = sq.astype(BF16)
            sq_lo = (sq - sq_hi.astype(F32)).astype(BF16)
            ms = _dot(jnp.concatenate([sq_hi, sq_lo], axis=1), ones_sum) * (1.0 / dv)
            y = hh * rep(lax.rsqrt(ms + EPS), v_tiles) * g_ref[:, h * dv:(h + 1) * dv]
            y = y * jax.nn.sigmoid(mo_ref[rows, h * dv:(h + 1) * dv])
            o_ref[rows, h * dv:(h + 1) * dv] = y.astype(o_ref.dtype)

        sv, m_new, carry_scale = [], [], []
        for h in hs:
            a_rep = arep_ref[rows, lanes(h)]
            b_last = brep_ref[tail, lanes(h)][7:8, :]
            m_new.append(jnp.maximum(b_last + m_prev[h], jnp.max(a_rep, axis=0, keepdims=True)))
            carry_scale.append(jnp.exp(b_last + m_prev[h] - m_new[h]))
            src = jnp.exp(a_rep - m_new[h])
            sv.append((rep(src, v_tiles + 1) * v_ext[h].astype(F32)).astype(BF16))
        upd = [lax.dot_general(k[h], sv[h], (((0,), (0,)), ((), ())), preferred_element_type=F32)
               for h in hs]
        for h in hs:
            cn_ref[h] = rep(carry_scale[h], v_tiles + 1) * cn[h] + upd[h]
            m_ref[h:h + 1, :] = m_new[h]
        return carry

    lax.fori_loop(0, n_chunks, chunk_body, 0)


def mlstm(q, k, v, v_col0, mo, mo_col0, b_rep, rmax_rep, a_rep, r_rows, g_out, heads, dk, dv, chunk):
    s = q.shape[0]
    rows = _pick(s, (512, 256, 128, 64))
    n_chunks = rows // chunk
    wv = heads * dv
    assert v_col0 % wv == 0 and mo_col0 % wv == 0 and dv % LANES == 0
    v_blk, mo_blk = v_col0 // wv, mo_col0 // wv
    rep_spec = pl.BlockSpec((rows, heads * LANES), lambda i: (i, 0))
    return pl.pallas_call(
        functools.partial(_mlstm_kernel, heads=heads, dk=dk, dv=dv, chunk=chunk, n_chunks=n_chunks),
        grid=(s // rows,),
        in_specs=[pl.BlockSpec((rows, heads * dk), lambda i: (i, 0)),
                  pl.BlockSpec((rows, heads * dk), lambda i: (i, 0)),
                  pl.BlockSpec((rows, wv), lambda i: (i, v_blk)),
                  pl.BlockSpec((rows, wv), lambda i: (i, mo_blk)),
                  rep_spec, rep_spec, rep_spec,
                  pl.BlockSpec((n_chunks, heads, chunk), lambda i: (i, 0, 0)),
                  pl.BlockSpec((1, wv), lambda i: (0, 0))],
        out_specs=pl.BlockSpec((rows, wv), lambda i: (i, 0)),
        out_shape=jax.ShapeDtypeStruct((s, wv), BF16),
        scratch_shapes=[pltpu.VMEM((heads, dk, dv + LANES), F32),
                        pltpu.VMEM((heads, LANES), F32)],
        compiler_params=_params(1),
        name="mlstm",
    )(q, k, v, mo, b_rep, rmax_rep, a_rep, r_rows, g_out.reshape(1, wv).astype(F32))


def _sb_kernel(q_ref, k_ref, v_ref, o_ref, acc_ref, run_ref, live_ref, *, tk, hp, d):
    i = pl.program_id(1)
    row = lax.broadcasted_iota(jnp.int32, (tk, tk), 0)
    col = lax.broadcasted_iota(jnp.int32, (tk, tk), 1)
    diag_strict = col < row
    sums_rhs = jnp.concatenate([(row > col).astype(BF16), jnp.ones((tk, tk), BF16)], axis=1)
    sums_rhs = jnp.concatenate([sums_rhs, sums_rhs], axis=0)

    def walk(tiles):
        zs = [_dot_nt(t[0], t[1]) for t in tiles]
        log_betas, splits = [], []
        for z, (_, _, _, _, mask, _) in zip(zs, tiles):
            log_beta = jnp.minimum(z, 0.0) - jnp.log(1.0 + jnp.exp(-jnp.abs(z)))
            log_rest = log_beta - z
            if mask is not None:
                log_rest = jnp.where(mask, log_rest, 0.0)
            hi = log_rest.astype(BF16)
            lo = (log_rest - hi.astype(F32)).astype(BF16)
            log_betas.append(log_beta)
            splits.append(jnp.concatenate([hi, lo], axis=1))
        sums = [_dot(x, sums_rhs) for x in splits]
        weights, runs = [], []
        for log_beta, sm, (_, _, _, run, mask, after) in zip(log_betas, sums, tiles):
            if after is not None:
                run = runs[after]
            log_w = log_beta + sm[:, :tk]
            wts = jnp.exp(log_w if run is None else log_w + run)
            if mask is not None:
                wts = jnp.where(mask, wts, 0.0)
            weights.append(wts.astype(BF16))
            runs.append(sm[:, tk:] if run is None else run + sm[:, tk:])
        outs = [_dot(w, t[2]) for w, t in zip(weights, tiles)]
        return list(zip(outs, runs))

    def cols(h):
        return slice(h * d, (h + 1) * d)

    def chain(near, length, run_of):
        tiles = []
        for t in range(length):
            blk = near - t
            key_rows = pl.ds(pl.multiple_of(jnp.maximum(blk, 0) * tk, tk), tk)
            exists = jnp.broadcast_to(blk >= 0, (tk, tk))
            for h in range(hp):
                kk, vv = k_ref[key_rows, cols(h)], v_ref[key_rows, cols(h)]
                if t > 0:
                    tiles.append((tiles[h][0], kk, vv, None, exists, (t - 1) * hp + h))
                elif run_of is None:
                    tiles.append((q_ref[:, cols(h)], kk, vv, None, diag_strict, None))
                else:
                    tiles.append((q_ref[:, cols(h)], kk, vv, run_of(h), None, None))
        res = walk(tiles)
        return [(functools.reduce(jnp.add, [res[t * hp + h][0] for t in range(length)]),
                 res[(length - 1) * hp + h][1]) for h in range(hp)]

    for h, (out, run) in enumerate(chain(i, SB_STATIC_BLOCKS, None)):
        acc_ref[h] = out
        run_ref[h] = run
    live_ref[0] = (jnp.max(run_ref[...]) > SB_DONE_LOG).astype(jnp.int32)

    first = i - SB_STATIC_BLOCKS
    n_pairs = (jnp.maximum(first, -1) + 2) // 2

    def pair_body(step, carry):
        @pl.when(live_ref[0] > 0)
        def _():
            for h, (out, run) in enumerate(chain(first - 2 * step, 2, lambda h: run_ref[h])):
                acc_ref[h] += out
                run_ref[h] = run
            live_ref[0] = (jnp.max(run_ref[...]) > SB_DONE_LOG).astype(jnp.int32)
        return carry

    lax.fori_loop(0, n_pairs, pair_body, 0)
    for h in range(hp):
        o_ref[:, cols(h)] = acc_ref[h].astype(o_ref.dtype)


def stick_breaking(qkv, col0, heads, d):
    s = qkv.shape[0]
    tq = tk = SB_Q_ROWS
    hp = SB_HEADS_PER_STEP
    assert s % tq == 0 and heads % hp == 0 and col0 % (hp * d) == 0
    n_hg = heads // hp
    blk0 = col0 // (hp * d)
    once = pl.Buffered(1)
    return pl.pallas_call(
        functools.partial(_sb_kernel, tk=tk, hp=hp, d=d),
        grid=(n_hg, s // tq),
        in_specs=[pl.BlockSpec((tq, hp * d), lambda hg, i: (i, blk0 + hg)),
                  pl.BlockSpec((s, hp * d), lambda hg, i: (0, blk0 + n_hg + hg), pipeline_mode=once),
                  pl.BlockSpec((s, hp * d), lambda hg, i: (0, blk0 + 2 * n_hg + hg), pipeline_mode=once)],
        out_specs=pl.BlockSpec((tq, hp * d), lambda hg, i: (i, hg)),
        out_shape=jax.ShapeDtypeStruct((s, heads * d), BF16),
        scratch_shapes=[pltpu.VMEM((hp, tq, d), F32),
                        pltpu.VMEM((hp, tq, tk), F32),
                        pltpu.SMEM((1,), jnp.int32)],
        compiler_params=_params(2),
        name="stick_breaking",
    )(qkv, qkv, qkv)


def _ffn(x, xb, rstd, w1, w3, w2, next_gain):
    u, w2b = gateup(xb, rstd, w1, w3, w2)
    return matmul_residual(u, w2b, x, 0.5, tm_max=512, tn_max=512, next_gain=next_gain)


def _mixers(x, xb, rstd, w_in, conv_qk, b_igate, b_fgate, g_mlstm_out, w_proj_a, w_proj_b, w_out, next_gain):
    s, d_model = x.shape
    qk_w = ML_HEADS * ML_QK_DIM
    ml_w = ML_HEADS * ML_V_DIM
    sb_w = SB_HEADS * SB_HEAD_DIM
    o_mv = 2 * qk_w
    o_mo = o_mv + ml_w
    o_mi = o_mo + ml_w
    o_mf = o_mi + ML_HEADS
    o_sq = o_mf + ML_HEADS
    o_ga = o_sq + 3 * sb_w
    wt = jnp.swapaxes(w_in, 0, 1)
    zrows = jnp.zeros((LANES - ML_HEADS, d_model), F32)
    wt_gate = jnp.concatenate([wt[o_mi:o_mf], zrows, wt[o_mf:o_sq], zrows], axis=0)
    scale16 = jnp.concatenate([jnp.ones((ml_w,), F32),
                               jnp.full((sb_w,), SB_HEAD_DIM ** -0.5, F32),
                               jnp.ones((2 * sb_w,), F32)])

    p32 = project(xb, rstd, wt, [(0, o_mv), (o_mo, ml_w), (o_ga, 2 * d_model)], F32)
    p16, w_proj_a, w_proj_b, w_out = project(xb, rstd, wt, [(o_mv, ml_w), (o_sq, 3 * sb_w)], BF16,
                                             col_scale=scale16, cast_jobs=(w_proj_a, w_proj_b, w_out))
    pg = project(xb, rstd, wt_gate, [(0, 2 * LANES)], F32)

    q, k = conv_silu_qk(p32, 2 * qk_w, conv_qk, ML_QK_DIM ** -0.5)
    r, b_rep, rmax_rep, a_rep = gate_prep(pg, b_igate, b_fgate, ML_CHUNK, ML_HEADS)
    r_rows = jnp.swapaxes(r[:, :ML_HEADS].reshape(s // ML_CHUNK, ML_CHUNK, ML_HEADS), 1, 2)
    ya = mlstm(q, k, p16, 0, p32, 2 * qk_w, b_rep, rmax_rep, a_rep, r_rows, g_mlstm_out,
               ML_HEADS, ML_QK_DIM, ML_V_DIM, ML_CHUNK)

    yb = stick_breaking(p16, ml_w, SB_HEADS, SB_HEAD_DIM)

    merged = gated_merge(ya, yb, w_proj_a, w_proj_b, p32, 2 * qk_w + ml_w, 2 * qk_w + ml_w + d_model)
    return matmul_residual(merged, w_out, x, 1.0, tm_max=1024, tn_max=512, next_gain=next_gain)


def kernel(x, g_ffn1, w1_ffn1, w3_ffn1, w2_ffn1, g_mix, w_in, conv_qk, b_igate, b_fgate,
           g_mlstm_out, w_proj_a, w_proj_b, w_out, g_ffn2, w1_ffn2, w3_ffn2, w2_ffn2, g_final):
    batch, seq, d_model = x.shape
    outs = []
    depth = g_ffn1.shape[0]
    for bi in range(batch):
        xf = x[bi]
        xb, rstd = cast_stats(xf, g_ffn1[0])
        for l in range(depth):
            xf, xb, rstd = _ffn(xf, xb, rstd, w1_ffn1[l], w3_ffn1[l], w2_ffn1[l], g_mix[l])
            xf, xb, rstd = _mixers(xf, xb, rstd, w_in[l], conv_qk[l], b_igate[l], b_fgate[l],
                                   g_mlstm_out[l], w_proj_a[l], w_proj_b[l], w_out[l], g_ffn2[l])
            if l + 1 < depth:
                xf, xb, rstd = _ffn(xf, xb, rstd, w1_ffn2[l], w3_ffn2[l], w2_ffn2[l], g_ffn1[l + 1])
            else:
                xf = _ffn(xf, xb, rstd, w1_ffn2[l], w3_ffn2[l], w2_ffn2[l], None)
        outs.append(rmsnorm(xf, g_final, x.dtype))
    return jnp.stack(outs, axis=0)
```
